```python
import jax
import jax.numpy as jnp
from jax import lax
import numpy as np

D_MODEL = 2048
BATCH = 16
SEQ = 2048
DEPTH = 1

GRID_W = 64
CTX_LEN = 256
EPS = 1e-6
N_MOD = 6

GLA_HEADS = 4
GLA_DK = D_MODEL // 2
GLA_DV = D_MODEL
GLA_HK = GLA_DK // GLA_HEADS
GLA_HV = GLA_DV // GLA_HEADS
GLA_RANK = 16
GLA_NORMALIZER = 16.0
GLA_CHUNK = 64

SSD_DI = 2 * D_MODEL
SSD_HEADDIM = 64
SSD_HEADS = SSD_DI // SSD_HEADDIM
SSD_GROUPS = 8
SSD_HPG = SSD_HEADS // SSD_GROUPS
SSD_STATE = 128
SSD_CONV = 3
SSD_CHUNK = 128
SSD_CONV_CH = SSD_DI + 2 * SSD_GROUPS * SSD_STATE

PEER_HEADS = 8
PEER_NKEYS = 128
PEER_EXPERTS = PEER_NKEYS * PEER_NKEYS
PEER_DKEY = 256
PEER_TOPK = 16
PEER_BLOCK = 128

IN_SIZES = (GLA_DK, GLA_DK, GLA_DV, GLA_DV, GLA_RANK, GLA_RANK,
            SSD_DI, SSD_CONV_CH, SSD_HEADS, SSD_HEADS, 2 * D_MODEL)
IN_COLS = sum(IN_SIZES)
IN_SPLITS = [int(s) for s in np.cumsum(IN_SIZES)[:-1]]

kernel_name = 'hybrid_gla_ssd_peer_prefix_block'


def rmsnorm(x, w):
    xf = x.astype(jnp.float32)
    y = xf * lax.rsqrt(jnp.mean(xf * xf, axis=-1, keepdims=True) + EPS)
    return (y * w.astype(jnp.float32)).astype(x.dtype)


def modulate(h, shift, scale):
    return h * (1.0 + scale) + shift


def _chunks(t, size):
    b, l = t.shape[:2]
    return jnp.moveaxis(t.reshape(b, l // size, size, *t.shape[2:]), 1, 0)


def _unchunk(t):
    t = jnp.moveaxis(t, 0, 1)
    return t.reshape(t.shape[0], t.shape[1] * t.shape[2], *t.shape[3:])


def _flip(t):
    return jnp.flip(t, axis=1)


def gla_scan(q, k, v, g, s0):
    mask = jnp.tril(jnp.ones((GLA_CHUNK, GLA_CHUNK), dtype=bool))

    def step(s, inp):
        qc, kc, vc, gc = inp
        b = jnp.cumsum(gc, axis=1)
        b_last = b[:, -1]
        qd = qc * jnp.exp(b)
        kd = kc * jnp.exp(-b)
        a = jnp.where(mask, jnp.einsum('bihk,bjhk->bhij', qd, kd), 0.0)
        o = jnp.einsum('bhij,bjhv->bihv', a, vc) + jnp.einsum('bihk,bhkv->bihv', qd, s)
        kt = kc * jnp.exp(b_last[:, None] - b)
        s = s * jnp.exp(b_last)[..., None] + jnp.einsum('bjhk,bjhv->bhkv', kt, vc)
        return s, o

    s, o = lax.scan(step, s0, (_chunks(q, GLA_CHUNK), _chunks(k, GLA_CHUNK),
                               _chunks(v, GLA_CHUNK), _chunks(g, GLA_CHUNK)))
    return _unchunk(o), s


def ssd_scan(xs, dt, a, bm, cm, s0):
    mask = jnp.tril(jnp.ones((SSD_CHUNK, SSD_CHUNK), dtype=bool))

    def step(s, inp):
        xc, dtc, bc, cc = inp
        cum = jnp.cumsum(dtc * a, axis=1)
        cum_t = jnp.moveaxis(cum, 1, -1)
        seg = cum_t[..., :, None] - cum_t[..., None, :]
        decay = jnp.exp(jnp.where(mask, seg, -jnp.inf))
        cb = jnp.einsum('bign,bjgn->bgij', cc, bc)
        xdt = xc * dtc[..., None]
        y = jnp.einsum('bghij,bjghp->bighp', cb[:, :, None] * decay, xdt)
        y = y + jnp.einsum('bign,bghnp->bighp', cc, s) * jnp.exp(cum)[..., None]
        c_last = cum[:, -1]
        w = jnp.exp(c_last[:, None] - cum)[..., None] * xdt
        s = s * jnp.exp(c_last)[..., None, None] + jnp.einsum('bjgn,bjghp->bghnp', bc, w)
        return s, y

    s, y = lax.scan(step, s0, (_chunks(xs, SSD_CHUNK), _chunks(dt, SSD_CHUNK),
                               _chunks(bm, SSD_CHUNK), _chunks(cm, SSD_CHUNK)))
    return _unchunk(y), s


def dwconv2d(t, rows, w, b):
    bn, l, ch = t.shape
    img = t.reshape(bn, rows, l // rows, ch)
    y = lax.conv_general_dilated(
        img, w.reshape(SSD_CONV, SSD_CONV, 1, ch).astype(t.dtype), (1, 1), 'SAME',
        dimension_numbers=('NHWC', 'HWIO', 'NHWC'), feature_group_count=ch)
    return y.reshape(bn, l, ch) + b


def branch_core(h, rows, p, init):
    bn, l, _ = h.shape
    f32 = jnp.float32
    (q, k, v, r, lr_f, lr_b, z, xbc, dt_f, dt_b, gl) = jnp.split(h @ p['w_in'], IN_SPLITS, axis=-1)

    qh = q.reshape(bn, l, GLA_HEADS, GLA_HK).astype(f32) * GLA_HK ** -0.5
    kh = k.reshape(bn, l, GLA_HEADS, GLA_HK).astype(f32)
    vh = v.reshape(bn, l, GLA_HEADS, GLA_HV).astype(f32)

    def log_decay(lr, w2, b2):
        logit = (lr @ w2 + b2).astype(f32)
        return (jax.nn.log_sigmoid(logit) / GLA_NORMALIZER).reshape(bn, l, GLA_HEADS, GLA_HK)

    g_f = log_decay(lr_f, p['w_lr2_f'], p['b_lr_f'])
    g_b = log_decay(lr_b, p['w_lr2_b'], p['b_lr_b'])

    xbc = jax.nn.silu(dwconv2d(xbc, rows, p['conv_w'], p['conv_b']))
    xs, bm, cm = jnp.split(xbc, [SSD_DI, SSD_DI + SSD_GROUPS * SSD_STATE], axis=-1)
    xs = xs.reshape(bn, l, SSD_GROUPS, SSD_HPG, SSD_HEADDIM).astype(f32)
    bm = bm.reshape(bn, l, SSD_GROUPS, SSD_STATE).astype(f32)
    cm = cm.reshape(bn, l, SSD_GROUPS, SSD_STATE).astype(f32)

    def dt_act(dt, bias):
        return jax.nn.softplus((dt + bias).astype(f32)).reshape(bn, l, SSD_GROUPS, SSD_HPG)

    dtf = dt_act(dt_f, p['dt_bias_f'])
    dtb = dt_act(dt_b, p['dt_bias_b'])
    a_f = -jnp.exp(p['a_log_f'].astype(f32)).reshape(SSD_GROUPS, SSD_HPG)
    a_b = -jnp.exp(p['a_log_b'].astype(f32)).reshape(SSD_GROUPS, SSD_HPG)

    if init is None:
        zg = jnp.zeros((bn, GLA_HEADS, GLA_HK, GLA_HV), f32)
        zs = jnp.zeros((bn, SSD_GROUPS, SSD_HPG, SSD_STATE, SSD_HEADDIM), f32)
        init = (zg, zg, zs, zs)
    s_gf, s_gb, s_sf, s_sb = init

    o_f, s_gf = gla_scan(qh, kh, vh, g_f, s_gf)
    o_b, s_gb = gla_scan(_flip(qh), _flip(kh), _flip(vh), _flip(g_b), s_gb)
    o_gla = o_f + _flip(o_b)

    y_f, s_sf = ssd_scan(xs, dtf, a_f, bm, cm, s_sf)
    y_b, s_sb = ssd_scan(_flip(xs), _flip(dtb), a_b, _flip(bm), _flip(cm), s_sb)
    d = p['d_skip'].astype(f32).reshape(SSD_GROUPS, SSD_HPG)[..., None]
    y_ssd = y_f + _flip(y_b) + d * xs
    return (o_gla, r, y_ssd, z, gl), (s_gf, s_gb, s_sf, s_sb)


def branch_out(pieces, p):
    o_gla, r, y_ssd, z, gl = pieces
    bn, l = r.shape[:2]
    dtype = r.dtype
    f32 = jnp.float32
    o = o_gla * lax.rsqrt(jnp.mean(o_gla * o_gla, axis=-1, keepdims=True) + EPS)
    o = (o * p['gla_norm_w'].astype(f32)).reshape(bn, l, GLA_DV) * jax.nn.silu(r.astype(f32))
    y_a = o.astype(dtype) @ p['w_gla_out']
    y = y_ssd.reshape(bn, l, SSD_DI) * jax.nn.silu(z.astype(f32))
    y = y.reshape(bn, l, SSD_GROUPS, SSD_DI // SSD_GROUPS)
    y = y * lax.rsqrt(jnp.mean(y * y, axis=-1, keepdims=True) + EPS)
    y = y.reshape(bn, l, SSD_DI) * p['ssd_norm_w'].astype(f32)
    y_b = y.astype(dtype) @ p['w_ssd_out']
    ga, gb = jnp.split(jax.nn.sigmoid(gl + p['b_gate']), 2, axis=-1)
    return (ga * y_a + gb * y_b) @ p['w_o']


def peer(h, w_q, sub_keys, u, v):
    bn, l, d = h.shape
    t = h.reshape(-1, d)
    n_tok = t.shape[0]
    q = (t @ w_q).astype(jnp.float32).reshape(n_tok, PEER_HEADS, 2, PEER_DKEY // 2)
    s = jnp.einsum('thpd,hpkd->thpk', q, sub_keys.astype(jnp.float32))
    sv, si = lax.top_k(s, PEER_TOPK)
    cand = sv[:, :, 0, :, None] + sv[:, :, 1, None, :]
    cv, ci = lax.top_k(cand.reshape(n_tok, PEER_HEADS, PEER_TOPK * PEER_TOPK), PEER_TOPK)
    i1 = jnp.take_along_axis(si[:, :, 0], ci // PEER_TOPK, axis=-1)
    i2 = jnp.take_along_axis(si[:, :, 1], ci % PEER_TOPK, axis=-1)
    idx = i1 * PEER_NKEYS + i2
    gate = jax.nn.softmax(cv, axis=-1)
    nb = n_tok // PEER_BLOCK

    def block(args):
        tb, ib, gb = args
        ue = jnp.take(u, ib, axis=0)
        act = jax.nn.gelu(jnp.einsum('td,thkd->thk', tb, ue).astype(jnp.float32), approximate=False)
        ve = jnp.take(v, ib, axis=0)
        return jnp.einsum('thk,thkd->td', (act * gb).astype(tb.dtype), ve)

    y = lax.map(block, (t.reshape(nb, PEER_BLOCK, d),
                        idx.reshape(nb, PEER_BLOCK, PEER_HEADS, PEER_TOPK),
                        gate.reshape(nb, PEER_BLOCK, PEER_HEADS, PEER_TOPK)))
    return y.reshape(bn, l, d)


def setup_inputs(seed: int = 0) -> dict:
    key = jax.random.key(seed)
    ks = iter(jax.random.split(key, 40))
    f32 = jnp.float32

    def nrm(shape, scale):
        return jax.random.normal(next(ks), shape, f32) * scale

    def dt_bias(shape):
        dt = jnp.exp(jax.random.uniform(next(ks), shape, f32, float(np.log(1e-3)), float(np.log(1e-1))))
        return dt + jnp.log(-jnp.expm1(-dt))

    def a_log(shape):
        return jnp.log(jax.random.uniform(next(ks), shape, f32, 1.0, 16.0))

    D = D_MODEL
    return {
        'x': nrm((BATCH, SEQ, D), 1.0),
        'c': nrm((BATCH, D), 1.0),
        'ctx': nrm((BATCH, CTX_LEN, D), 1.0),
        'c_ctx': nrm((D,), 1.0),
        'w_ada': nrm((DEPTH, D, N_MOD * D), 0.5 * D ** -0.5),
        'b_ada': nrm((DEPTH, N_MOD * D), 0.01),
        'norm1_w': 1.0 + nrm((DEPTH, D), 0.01),
        'w_in': nrm((DEPTH, D, IN_COLS), D ** -0.5),
        'b_gate': nrm((DEPTH, 2 * D), 0.01),
        'w_lr2_f': nrm((DEPTH, GLA_RANK, GLA_DK), GLA_RANK ** -0.5),
        'b_lr_f': nrm((DEPTH, GLA_DK), 0.01),
        'w_lr2_b': nrm((DEPTH, GLA_RANK, GLA_DK), GLA_RANK ** -0.5),
        'b_lr_b': nrm((DEPTH, GLA_DK), 0.01),
        'gla_norm_w': 1.0 + nrm((DEPTH, GLA_HV), 0.01),
        'w_gla_out': nrm((DEPTH, GLA_DV, D), GLA_DV ** -0.5),
        'conv_w': nrm((DEPTH, SSD_CONV, SSD_CONV, SSD_CONV_CH), 1.0 / SSD_CONV),
        'conv_b': nrm((DEPTH, SSD_CONV_CH), 0.01),
        'a_log_f': a_log((DEPTH, SSD_HEADS)),
        'a_log_b': a_log((DEPTH, SSD_HEADS)),
        'dt_bias_f': dt_bias((DEPTH, SSD_HEADS)),
        'dt_bias_b': dt_bias((DEPTH, SSD_HEADS)),
        'd_skip': 1.0 + nrm((DEPTH, SSD_HEADS), 0.01),
        'ssd_norm_w': 1.0 + nrm((DEPTH, SSD_DI), 0.01),
        'w_ssd_out': nrm((DEPTH, SSD_DI, D), SSD_DI ** -0.5),
        'w_o': nrm((DEPTH, D, D), D ** -0.5),
        'norm2_w': 1.0 + nrm((DEPTH, D), 0.01),
        'peer_wq': nrm((DEPTH, D, PEER_HEADS * PEER_DKEY), D ** -0.5),
        'peer_keys': nrm((DEPTH, PEER_HEADS, 2, PEER_NKEYS, PEER_DKEY // 2), (PEER_DKEY // 2) ** -0.5),
        'peer_u': nrm((DEPTH, PEER_EXPERTS, D), D ** -0.5),
        'peer_v': nrm((DEPTH, PEER_EXPERTS, D), PEER_HEADS ** -0.5),
        'final_norm_w': 1.0 + nrm((D,), 0.01),
    }


def reference(x, c, ctx, c_ctx, w_ada, b_ada, norm1_w, w_in, b_gate, w_lr2_f, b_lr_f,
              w_lr2_b, b_lr_b, gla_norm_w, w_gla_out, conv_w, conv_b, a_log_f, a_log_b,
              dt_bias_f, dt_bias_b, d_skip, ssd_norm_w, w_ssd_out, w_o, norm2_w,
              peer_wq, peer_keys, peer_u, peer_v, final_norm_w):
    rows = x.shape[1] // GRID_W
    for layer in range(DEPTH):
        p = {
            'w_in': w_in[layer], 'b_gate': b_gate[layer],
            'w_lr2_f': w_lr2_f[layer], 'b_lr_f': b_lr_f[layer],
            'w_lr2_b': w_lr2_b[layer], 'b_lr_b': b_lr_b[layer],
            'gla_norm_w': gla_norm_w[layer], 'w_gla_out': w_gla_out[layer],
            'conv_w': conv_w[layer], 'conv_b': conv_b[layer],
            'a_log_f': a_log_f[layer], 'a_log_b': a_log_b[layer],
            'dt_bias_f': dt_bias_f[layer], 'dt_bias_b': dt_bias_b[layer],
            'd_skip': d_skip[layer], 'ssd_norm_w': ssd_norm_w[layer],
            'w_ssd_out': w_ssd_out[layer], 'w_o': w_o[layer],
        }
        mod_x = jnp.split((jax.nn.silu(c) @ w_ada[layer] + b_ada[layer])[:, None, :], N_MOD, axis=-1)
        mod_c = jnp.split((jax.nn.silu(c_ctx) @ w_ada[layer] + b_ada[layer])[None, None, :], N_MOD, axis=-1)

        hc = modulate(rmsnorm(ctx, norm1_w[layer]), mod_c[0], mod_c[1])
        hx = modulate(rmsnorm(x, norm1_w[layer]), mod_x[0], mod_x[1])
        core_c, ctx_states = branch_core(hc, 1, p, None)
        core_x, _ = branch_core(hx, rows, p, ctx_states)
        x = x + mod_x[2] * branch_out(core_x, p)

        hx2 = modulate(rmsnorm(x, norm2_w[layer]), mod_x[3], mod_x[4])
        x = x + mod_x[5] * peer(hx2, peer_wq[layer], peer_keys[layer], peer_u[layer], peer_v[layer])

        if layer < DEPTH - 1:
            ctx = ctx + mod_c[2] * branch_out(core_c, p)
            hc2 = modulate(rmsnorm(ctx, norm2_w[layer]), mod_c[3], mod_c[4])
            ctx = ctx + mod_c[5] * peer(hc2, peer_wq[layer], peer_keys[layer], peer_u[layer], peer_v[layer])
    return rmsnorm(x, final_norm_w)
```

```python
import functools

import jax
import jax.numpy as jnp
from jax import lax
from jax.experimental import pallas as pl
from jax.experimental.pallas import tpu as pltpu

F32 = jnp.float32
BF16 = jnp.bfloat16
EPS = 1e-6
HIGHEST = lax.Precision.HIGHEST

N_MOD = 6
GRID_W = 64
GLA_HEADS = 4
GLA_RANK = 16
GLA_NORMALIZER = 16.0
GLA_CHUNK = 64
SSD_HEADDIM = 64
SSD_GROUPS = 8
SSD_HPG = 8
SSD_STATE = 128
SSD_CHUNK = 128
PEER_HEADS = 8
PEER_NKEYS = 128
PEER_TOPK = 16

VMEM_LIMIT_BYTES = 56 * 1024 * 1024
LANES = 128

NT_DIMS = (((1,), (1,)), ((), ()))
TN_DIMS = (((0,), (0,)), ((), ()))


def _cparams(*sem):
    return pltpu.CompilerParams(dimension_semantics=sem, vmem_limit_bytes=VMEM_LIMIT_BYTES)


def _pick(n, options):
    for o in options:
        if n % o == 0:
            return o
    raise ValueError(f"no tile in {options} divides {n}")


def _softplus(x):
    return jnp.maximum(x, 0.0) + jnp.log1p(jnp.exp(-jnp.abs(x)))


def _silu(x):
    return x * jax.nn.sigmoid(x)


def _ada_kernel(c_ref, w_ref, b_ref, o_ref):
    a = _silu(c_ref[...])
    o_ref[...] = jnp.dot(a, w_ref[...], precision=HIGHEST, preferred_element_type=F32) + b_ref[...]


def _ada(c_all, w_ada, b_ada):
    m, d = c_all.shape
    n = w_ada.shape[1]
    bn = _pick(n, (1024, 512, 256, 128))
    return pl.pallas_call(
        _ada_kernel,
        grid=(n // bn,),
        in_specs=[pl.BlockSpec((m, d), lambda j: (0, 0)),
                  pl.BlockSpec((d, bn), lambda j: (0, j)),
                  pl.BlockSpec((1, bn), lambda j: (0, j))],
        out_specs=pl.BlockSpec((m, bn), lambda j: (0, j)),
        out_shape=jax.ShapeDtypeStruct((m, n), F32),
        compiler_params=_cparams("parallel"),
        name="ada_mod",
    )(c_all, w_ada, b_ada.reshape(1, n))


def _norm_mod_kernel(x_ref, ctx_ref, w_ref, sh_ref, sc_ref, o_ref, *, n_lat_tiles):
    j = pl.program_id(1)

    def emit(src):
        xf = src[0]
        ms = jnp.mean(xf * xf, axis=-1, keepdims=True)
        y = xf * lax.rsqrt(ms + EPS) * w_ref[...]
        o_ref[0] = (y * (1.0 + sc_ref[0, 0]) + sh_ref[0, 0]).astype(o_ref.dtype)

    @pl.when(j < n_lat_tiles)
    def _():
        emit(x_ref)

    @pl.when(j >= n_lat_tiles)
    def _():
        emit(ctx_ref)


def _norm_mod(x, ctx, w, shift, scale):
    b, l, d = x.shape
    lc = ctx.shape[1]
    tn = _pick(lc, (256, 128))
    nl, nc = l // tn, lc // tn
    kern = functools.partial(_norm_mod_kernel, n_lat_tiles=nl)
    return pl.pallas_call(
        kern,
        grid=(b, nl + nc),
        in_specs=[pl.BlockSpec((1, tn, d), lambda i, j: (i, jnp.minimum(j, nl - 1), 0)),
                  pl.BlockSpec((1, tn, d), lambda i, j: (i, jnp.maximum(j - nl, 0), 0)),
                  pl.BlockSpec((1, d), lambda i, j: (0, 0)),
                  pl.BlockSpec((1, 1, 1, d), lambda i, j: (i, j // nl, 0, 0)),
                  pl.BlockSpec((1, 1, 1, d), lambda i, j: (i, j // nl, 0, 0))],
        out_specs=pl.BlockSpec((1, tn, d), lambda i, j: (i, j, 0)),
        out_shape=jax.ShapeDtypeStruct((b, l + lc, d), BF16),
        compiler_params=_cparams("parallel", "parallel"),
        name="norm_mod",
    )(x, ctx, w.reshape(1, d), shift, scale)


def _matmul_kernel(a_ref, b_ref, o_ref):
    o_ref[...] = jnp.dot(a_ref[...], b_ref[...], preferred_element_type=F32).astype(o_ref.dtype)


def _matmul(a, b, out_dtype, name):
    m, k = a.shape
    n = b.shape[1]
    bm = _pick(m, (1024, 512, 256))
    bn = _pick(n, (1024, 512, 256))
    return pl.pallas_call(
        _matmul_kernel,
        grid=(m // bm, n // bn),
        in_specs=[pl.BlockSpec((bm, k), lambda i, j: (i, 0)),
                  pl.BlockSpec((k, bn), lambda i, j: (0, j))],
        out_specs=pl.BlockSpec((bm, bn), lambda i, j: (i, j)),
        out_shape=jax.ShapeDtypeStruct((m, n), out_dtype),
        compiler_params=_cparams("parallel", "parallel"),
        name=name,
    )(a, b)


CONV_TOK = 256
CTX_HALO = 8
LAT_HALO = GRID_W + 8


def _conv_kernel(x_ref, w_ref, b_ref, o_ref, lat_ref, ctx_ref, *, l, lc):
    tc = x_ref.shape[2]
    lat_ref[0:LAT_HALO, :] = jnp.zeros((LAT_HALO, tc), F32)
    lat_ref[LAT_HALO + l:LAT_HALO + l + LAT_HALO, :] = jnp.zeros((LAT_HALO, tc), F32)
    ctx_ref[0:CTX_HALO, :] = jnp.zeros((CTX_HALO, tc), F32)
    ctx_ref[CTX_HALO + lc:CTX_HALO + lc + CTX_HALO, :] = jnp.zeros((CTX_HALO, tc), F32)
    lat_ref[LAT_HALO:LAT_HALO + l, :] = x_ref[0, 0:l, :].astype(F32)
    ctx_ref[CTX_HALO:CTX_HALO + lc, :] = x_ref[0, l:l + lc, :].astype(F32)

    bias = b_ref[...]
    col = lax.broadcasted_iota(jnp.int32, (CONV_TOK, tc), 0) % GRID_W
    not_first = col != 0
    not_last = col != GRID_W - 1

    for t0 in range(0, l, CONV_TOK):
        acc = jnp.zeros((CONV_TOK, tc), F32)
        for kw in range(3):
            part = jnp.zeros((CONV_TOK, tc), F32)
            for kh in range(3):
                start = LAT_HALO + t0 + GRID_W * (kh - 1) + (kw - 1)
                part = part + w_ref[kh * 3 + kw:kh * 3 + kw + 1, :] * lat_ref[start:start + CONV_TOK, :]
            if kw == 0:
                part = jnp.where(not_first, part, 0.0)
            elif kw == 2:
                part = jnp.where(not_last, part, 0.0)
            acc = acc + part
        o_ref[0, t0:t0 + CONV_TOK, :] = _silu(acc + bias).astype(o_ref.dtype)

    ctx_tok = min(CONV_TOK, lc)
    for t0 in range(0, lc, ctx_tok):
        acc = jnp.zeros((ctx_tok, tc), F32)
        for kw in range(3):
            start = CTX_HALO + t0 + (kw - 1)
            acc = acc + w_ref[3 + kw:4 + kw, :] * ctx_ref[start:start + ctx_tok, :]
        o_ref[0, l + t0:l + t0 + ctx_tok, :] = _silu(acc + bias).astype(o_ref.dtype)


def _conv(proj, col_block0, n_ch, conv_w, conv_b, l, lc):
    b, s, _ = proj.shape
    tc = 512
    kern = functools.partial(_conv_kernel, l=l, lc=lc)
    return pl.pallas_call(
        kern,
        grid=(b, n_ch // tc),
        in_specs=[pl.BlockSpec((1, s, tc), lambda i, j: (i, 0, col_block0 + j)),
                  pl.BlockSpec((9, tc), lambda i, j: (0, j)),
                  pl.BlockSpec((1, tc), lambda i, j: (0, j))],
        out_specs=pl.BlockSpec((1, s, tc), lambda i, j: (i, 0, j)),
        out_shape=jax.ShapeDtypeStruct((b, s, n_ch), BF16),
        scratch_shapes=[pltpu.VMEM((l + 2 * LAT_HALO, tc), F32),
                        pltpu.VMEM((lc + 2 * CTX_HALO, tc), F32)],
        compiler_params=_cparams("parallel", "parallel"),
        name="dwconv_silu",
    )(proj, conv_w.reshape(9, n_ch), conv_b.reshape(1, n_ch))


def _ssd_prep_kernel(x_ref, bias_ref, a_ref, dt_ref, cum_ref):
    c = SSD_CHUNK
    s = x_ref.shape[1]
    row = lax.broadcasted_iota(jnp.int32, (c, c), 0)
    colm = lax.broadcasted_iota(jnp.int32, (c, c), 1)
    tri_f = (row >= colm).astype(F32)
    tri_b = (row <= colm).astype(F32)
    is_fwd = lax.broadcasted_iota(jnp.int32, (c, LANES), 1) < (LANES // 2)
    for t0 in range(0, s, c):
        dt = _softplus(x_ref[0, t0:t0 + c, :] + bias_ref[...])
        dta = dt * a_ref[...]
        cf = jnp.dot(tri_f, dta, precision=HIGHEST, preferred_element_type=F32)
        cb = jnp.dot(tri_b, dta, precision=HIGHEST, preferred_element_type=F32)
        dt_ref[0, t0:t0 + c, :] = dt
        cum_ref[0, t0:t0 + c, :] = jnp.where(is_fwd, cf, cb)


def _ssd_prep(small, bias, a):
    b, s, _ = small.shape
    out = jax.ShapeDtypeStruct((b, s, LANES), F32)
    return pl.pallas_call(
        _ssd_prep_kernel,
        grid=(b,),
        in_specs=[pl.BlockSpec((1, s, LANES), lambda i: (i, 0, 0)),
                  pl.BlockSpec((1, LANES), lambda i: (0, 0)),
                  pl.BlockSpec((1, LANES), lambda i: (0, 0))],
        out_specs=[pl.BlockSpec((1, s, LANES), lambda i: (i, 0, 0)),
                   pl.BlockSpec((1, s, LANES), lambda i: (i, 0, 0))],
        out_shape=[out, out],
        compiler_params=_cparams("parallel"),
        name="ssd_prep",
    )(small, bias, a)


def _ssd_kernel(xs_ref, bm_ref, cm_ref, z_ref, cumc_ref, cumr_ref, dtr_ref, dsk_ref, nw_ref,
                o_ref, s_ref, y_ref, *, l, lc):
    c = SSD_CHUNK
    p = SSD_HEADDIM
    nh = SSD_HPG
    n_lat, n_ctx = l // c, lc // c
    row = lax.broadcasted_iota(jnp.int32, (c, c), 0)
    colm = lax.broadcasted_iota(jnp.int32, (c, c), 1)

    def chunk(ci, d, emit):
        off = pl.multiple_of(ci * c, c)
        tok = pl.ds(off, c)
        x = xs_ref[0, tok, :]
        bmat = bm_ref[0, tok, :]
        bt = bmat.astype(F32).T
        cumr = cumr_ref[0, d, ci]
        dtr = dtr_ref[0, d, ci]
        last = c - 1 if d == 0 else 0
        if emit:
            cmat = cm_ref[0, tok, :]
            cb = lax.dot_general(cmat, bmat, NT_DIMS, preferred_element_type=F32)
            cumc = cumc_ref[0, d, tok, :]
            mask = (row >= colm) if d == 0 else (row <= colm)
            ys = []
        for h in range(nh):
            hs = slice(h * p, (h + 1) * p)
            cr = cumr[h:h + 1, :]
            dr = dtr[h:h + 1, :]
            cl = cr[:, last:last + 1]
            xh = x[:, hs]
            sh = s_ref[:, hs]
            if emit:
                cc = cumc[:, h:h + 1]
                decay = jnp.exp(jnp.where(mask, cc - cr, -jnp.inf))
                m = (cb * decay * dr).astype(BF16)
                y = jnp.dot(m, xh, preferred_element_type=F32)
                y = y + jnp.dot(cmat, sh.astype(BF16), preferred_element_type=F32) * jnp.exp(cc)
                ys.append(y)
            w = jnp.exp(cl - cr) * dr
            s_ref[:, hs] = sh * jnp.exp(cl) + jnp.dot((bt * w).astype(BF16), xh,
                                                      preferred_element_type=F32)
        if emit:
            y = jnp.concatenate(ys, axis=1)
            if d == 0:
                y_ref[tok, :] = y + dsk_ref[0] * x.astype(F32)
            else:
                y_ref[tok, :] = y_ref[tok, :] + y

    for d in range(2):
        s_ref[...] = jnp.zeros_like(s_ref)
        if d == 0:
            lax.fori_loop(0, n_ctx, lambda i, carry: (chunk(n_lat + i, 0, False), carry)[1], 0)
            lax.fori_loop(0, n_lat, lambda i, carry: (chunk(i, 0, True), carry)[1], 0)
        else:
            lax.fori_loop(0, n_ctx, lambda i, carry: (chunk(n_lat + n_ctx - 1 - i, 1, False), carry)[1], 0)
            lax.fori_loop(0, n_lat, lambda i, carry: (chunk(n_lat - 1 - i, 1, True), carry)[1], 0)

    rows = 256
    for t0 in range(0, l, rows):
        y = y_ref[t0:t0 + rows, :] * _silu(z_ref[0, t0:t0 + rows, :].astype(F32))
        ms = jnp.mean(y * y, axis=-1, keepdims=True)
        o_ref[0, t0:t0 + rows, :] = (y * lax.rsqrt(ms + EPS) * nw_ref[0]).astype(o_ref.dtype)


def _ssd(xbc, proj, z_block0, cumc, cumr, dtr, dskip, norm_w, l, lc):
    b, s, _ = xbc.shape
    g = SSD_GROUPS
    gw = SSD_HPG * SSD_HEADDIM
    nchunks = s // SSD_CHUNK
    xs_blocks = g * gw // SSD_STATE
    kern = functools.partial(_ssd_kernel, l=l, lc=lc)
    return pl.pallas_call(
        kern,
        grid=(b, g),
        in_specs=[pl.BlockSpec((1, s, gw), lambda i, j: (i, 0, j)),
                  pl.BlockSpec((1, s, SSD_STATE), lambda i, j: (i, 0, xs_blocks + j)),
                  pl.BlockSpec((1, s, SSD_STATE), lambda i, j: (i, 0, xs_blocks + g + j)),
                  pl.BlockSpec((1, l, gw), lambda i, j: (i, 0, z_block0 + j)),
                  pl.BlockSpec((1, 2, s, SSD_HPG), lambda i, j: (i * g + j, 0, 0, 0)),
                  pl.BlockSpec((1, 2, nchunks, SSD_HPG, SSD_CHUNK), lambda i, j: (i * g + j, 0, 0, 0, 0)),
                  pl.BlockSpec((1, 2, nchunks, SSD_HPG, SSD_CHUNK), lambda i, j: (i * g + j, 0, 0, 0, 0)),
                  pl.BlockSpec((1, 1, gw), lambda i, j: (j, 0, 0)),
                  pl.BlockSpec((1, 1, gw), lambda i, j: (j, 0, 0))],
        out_specs=pl.BlockSpec((1, l, gw), lambda i, j: (i, 0, j)),
        out_shape=jax.ShapeDtypeStruct((b, l, g * gw), BF16),
        scratch_shapes=[pltpu.VMEM((SSD_STATE, gw), F32),
                        pltpu.VMEM((l, gw), F32)],
        compiler_params=_cparams("parallel", "parallel"),
        name="ssd_scan",
    )(xbc, xbc, xbc, proj, cumc, cumr, dtr, dskip, norm_w)


def _gla_kernel(q_ref, k_ref, v_ref, r_ref, lr_ref, w2f_ref, b2f_ref, w2b_ref, b2b_ref, nw_ref,
                o_ref, st_ref, y_ref, *, l, lc):
    c = GLA_CHUNK
    n_lat, n_ctx = l // c, lc // c
    hk = q_ref.shape[2]
    scale = hk ** -0.5
    row = lax.broadcasted_iota(jnp.int32, (c, c), 0)
    colm = lax.broadcasted_iota(jnp.int32, (c, c), 1)

    def chunk(ci, d, emit):
        off = pl.multiple_of(ci * c, c)
        tok = pl.ds(off, c)
        mask = (row >= colm) if d == 0 else (row <= colm)
        last = c - 1 if d == 0 else 0
        w2 = w2f_ref if d == 0 else w2b_ref
        b2 = b2f_ref if d == 0 else b2b_ref
        lr = lr_ref[0, tok, :][:, d * GLA_RANK:(d + 1) * GLA_RANK]
        logit = jnp.dot(lr, w2[...], precision=HIGHEST, preferred_element_type=F32) + b2[...]
        g = -_softplus(-logit) * (1.0 / GLA_NORMALIZER)
        bcum = jnp.dot(mask.astype(F32), g, precision=HIGHEST, preferred_element_type=F32)
        b_last = bcum[last:last + 1, :]
        kf = k_ref[0, tok, :].astype(F32)
        v = v_ref[0, tok, :]
        if emit:
            qd = (q_ref[0, tok, :].astype(F32) * scale * jnp.exp(bcum)).astype(BF16)
            kd = (kf * jnp.exp(-bcum)).astype(BF16)
            a = lax.dot_general(qd, kd, NT_DIMS, preferred_element_type=F32)
            a = jnp.where(mask, a, 0.0).astype(BF16)
            o = jnp.dot(a, v, preferred_element_type=F32)
            o = o + lax.dot_general(qd, st_ref[...].astype(BF16), NT_DIMS, preferred_element_type=F32)
            if d == 0:
                y_ref[tok, :] = o
            else:
                y_ref[tok, :] = y_ref[tok, :] + o
        kt = (kf * jnp.exp(b_last - bcum)).astype(BF16)
        st_ref[...] = st_ref[...] * jnp.exp(b_last) + lax.dot_general(
            v, kt, TN_DIMS, preferred_element_type=F32)

    for d in range(2):
        st_ref[...] = jnp.zeros_like(st_ref)
        if d == 0:
            lax.fori_loop(0, n_ctx, lambda i, carry: (chunk(n_lat + i, 0, False), carry)[1], 0)
            lax.fori_loop(0, n_lat, lambda i, carry: (chunk(i, 0, True), carry)[1], 0)
        else:
            lax.fori_loop(0, n_ctx, lambda i, carry: (chunk(n_lat + n_ctx - 1 - i, 1, False), carry)[1], 0)
            lax.fori_loop(0, n_lat, lambda i, carry: (chunk(n_lat - 1 - i, 1, True), carry)[1], 0)

    rows = 256
    for t0 in range(0, l, rows):
        o = y_ref[t0:t0 + rows, :]
        ms = jnp.mean(o * o, axis=-1, keepdims=True)
        o = o * lax.rsqrt(ms + EPS) * nw_ref[...]
        o_ref[0, t0:t0 + rows, :] = (o * _silu(r_ref[0, t0:t0 + rows, :].astype(F32))).astype(o_ref.dtype)


def _gla(proj, small, w2f, b2f, w2b, b2b, norm_w, l, lc, blocks):
    b, s, _ = proj.shape
    h = GLA_HEADS
    hk = w2f.shape[1] // h
    hv = norm_w.shape[0]
    q0, k0, v0, r0 = blocks
    kern = functools.partial(_gla_kernel, l=l, lc=lc)
    return pl.pallas_call(
        kern,
        grid=(b, h),
        in_specs=[pl.BlockSpec((1, s, hk), lambda i, j: (i, 0, q0 // hk + j)),
                  pl.BlockSpec((1, s, hk), lambda i, j: (i, 0, k0 // hk + j)),
                  pl.BlockSpec((1, s, hv), lambda i, j: (i, 0, v0 // hv + j)),
                  pl.BlockSpec((1, l, hv), lambda i, j: (i, 0, r0 // hv + j)),
                  pl.BlockSpec((1, s, LANES), lambda i, j: (i, 0, 1)),
                  pl.BlockSpec((GLA_RANK, hk), lambda i, j: (0, j)),
                  pl.BlockSpec((1, hk), lambda i, j: (0, j)),
                  pl.BlockSpec((GLA_RANK, hk), lambda i, j: (0, j)),
                  pl.BlockSpec((1, hk), lambda i, j: (0, j)),
                  pl.BlockSpec((1, hv), lambda i, j: (0, 0))],
        out_specs=pl.BlockSpec((1, l, hv), lambda i, j: (i, 0, j)),
        out_shape=jax.ShapeDtypeStruct((b, l, h * hv), BF16),
        scratch_shapes=[pltpu.VMEM((hv, hk), F32),
                        pltpu.VMEM((l, hv), F32)],
        compiler_params=_cparams("parallel", "parallel"),
        name="gla_scan",
    )(proj, proj, proj, proj, small, w2f, b2f.reshape(1, -1), w2b, b2b.reshape(1, -1),
      norm_w.reshape(1, hv))


def _merge_kernel(a_ref, y_ref, wa_ref, wb_ref, gla_ref, glb_ref, ba_ref, bb_ref, o_ref):
    ya = jnp.dot(a_ref[0], wa_ref[...], preferred_element_type=F32)
    yb = jnp.dot(y_ref[0], wb_ref[...], preferred_element_type=F32)
    ga = jax.nn.sigmoid(gla_ref[0].astype(F32) + ba_ref[...])
    gb = jax.nn.sigmoid(glb_ref[0].astype(F32) + bb_ref[...])
    o_ref[0] = (ga * ya + gb * yb).astype(o_ref.dtype)


def _merge(a_n, y_n, w_a, w_b, proj, gl0, b_gate):
    b, l, ka = a_n.shape
    kb = y_n.shape[2]
    d = w_a.shape[1]
    bm = _pick(l, (512, 256))
    bn = 512
    nb = d // bn
    bg = b_gate.reshape(1, 2 * d)
    return pl.pallas_call(
        _merge_kernel,
        grid=(b, l // bm, nb),
        in_specs=[pl.BlockSpec((1, bm, ka), lambda i, t, j: (i, t, 0)),
                  pl.BlockSpec((1, bm, kb), lambda i, t, j: (i, t, 0)),
                  pl.BlockSpec((ka, bn), lambda i, t, j: (0, j)),
                  pl.BlockSpec((kb, bn), lambda i, t, j: (0, j)),
                  pl.BlockSpec((1, bm, bn), lambda i, t, j: (i, t, gl0 // bn + j)),
                  pl.BlockSpec((1, bm, bn), lambda i, t, j: (i, t, gl0 // bn + nb + j)),
                  pl.BlockSpec((1, bn), lambda i, t, j: (0, j)),
                  pl.BlockSpec((1, bn), lambda i, t, j: (0, nb + j))],
        out_specs=pl.BlockSpec((1, bm, bn), lambda i, t, j: (i, t, j)),
        out_shape=jax.ShapeDtypeStruct((b, l, d), BF16),
        compiler_params=_cparams("parallel", "parallel", "parallel"),
        name="branch_merge",
    )(a_n, y_n, w_a, w_b, proj, proj, bg, bg)


def _resid_kernel(m_ref, w_ref, x_ref, g_ref, o_ref):
    y = jnp.dot(m_ref[0], w_ref[...], preferred_element_type=F32)
    o_ref[0] = x_ref[0] + g_ref[0] * y


def _resid(m, w_o, x, gate):
    b, l, d = x.shape
    bm = _pick(l, (1024, 512, 256))
    bn = 512
    return pl.pallas_call(
        _resid_kernel,
        grid=(b, l // bm, d // bn),
        in_specs=[pl.BlockSpec((1, bm, d), lambda i, t, j: (i, t, 0)),
                  pl.BlockSpec((d, bn), lambda i, t, j: (0, j)),
                  pl.BlockSpec((1, bm, bn), lambda i, t, j: (i, t, j)),
                  pl.BlockSpec((1, 1, bn), lambda i, t, j: (i, 0, j))],
        out_specs=pl.BlockSpec((1, bm, bn), lambda i, t, j: (i, t, j)),
        out_shape=jax.ShapeDtypeStruct((b, l, d), F32),
        compiler_params=_cparams("parallel", "parallel", "parallel"),
        name="attn_resid",
    )(m, w_o, x, gate)


def _peer_q_kernel(x_ref, nw_ref, sh_ref, sc_ref, wq_ref, h_ref, q_ref, lhs_ref):
    @pl.when(pl.program_id(2) == 0)
    def _():
        xf = x_ref[0]
        ms = jnp.mean(xf * xf, axis=-1, keepdims=True)
        y = xf * lax.rsqrt(ms + EPS) * nw_ref[...]
        hb = (y * (1.0 + sc_ref[0]) + sh_ref[0]).astype(BF16)
        lhs_ref[...] = hb
        h_ref[0] = hb

    q_ref[0] = jnp.dot(lhs_ref[...], wq_ref[...], preferred_element_type=F32)


def _peer_q(x1, norm_w, shift, scale, wq):
    b, l, d = x1.shape
    n = wq.shape[1]
    bm = _pick(l, (512, 256))
    bn = 512
    return pl.pallas_call(
        _peer_q_kernel,
        grid=(b, l // bm, n // bn),
        in_specs=[pl.BlockSpec((1, bm, d), lambda i, t, j: (i, t, 0)),
                  pl.BlockSpec((1, d), lambda i, t, j: (0, 0)),
                  pl.BlockSpec((1, 1, d), lambda i, t, j: (i, 0, 0)),
                  pl.BlockSpec((1, 1, d), lambda i, t, j: (i, 0, 0)),
                  pl.BlockSpec((d, bn), lambda i, t, j: (0, j))],
        out_specs=[pl.BlockSpec((1, bm, d), lambda i, t, j: (i, t, 0)),
                   pl.BlockSpec((1, bm, bn), lambda i, t, j: (i, t, j))],
        out_shape=[jax.ShapeDtypeStruct((b, l, d), BF16),
                   jax.ShapeDtypeStruct((b, l, n), F32)],
        scratch_shapes=[pltpu.VMEM((bm, d), BF16)],
        compiler_params=_cparams("parallel", "parallel", "arbitrary"),
        name="peer_query",
    )(x1, norm_w.reshape(1, d), shift, scale, wq)


def _top_values(x, k):
    sub = lax.broadcasted_iota(jnp.int32, (k, x.shape[1]), 0)
    out = jnp.zeros((k, x.shape[1]), F32)
    for i in range(k):
        m = jnp.max(x, axis=0, keepdims=True)
        out = jnp.where(sub == i, m, out)
        x = jnp.where(x == m, -jnp.inf, x)
    return out


def _peer_score_kernel(q_ref, keys_ref, s1_ref, e1_ref, s2_ref, e2_ref, thr_ref):
    k = PEER_TOPK
    dk = keys_ref.shape[3]
    q = q_ref[...]
    s1 = lax.dot_general(keys_ref[0, 0], q[:, 0:dk], NT_DIMS, precision=HIGHEST,
                         preferred_element_type=F32)
    s2 = lax.dot_general(keys_ref[0, 1], q[:, dk:2 * dk], NT_DIMS, precision=HIGHEST,
                         preferred_element_type=F32)
    sv1 = _top_values(s1, k)
    sv2 = _top_values(s2, k)
    cand = jnp.concatenate([sv1[a:a + 1, :] + sv2 for a in range(k)], axis=0)
    cv = _top_values(cand, k)
    z = jnp.sum(jnp.exp(cv - cv[0:1, :]), axis=0, keepdims=True)
    s1_ref[0] = s1
    s2_ref[0] = s2
    e1_ref[0] = jnp.exp(s1 - sv1[0:1, :]) / z
    e2_ref[0] = jnp.exp(s2 - sv2[0:1, :])
    thr_ref[0] = jnp.broadcast_to(cv[k - 1:k, :], thr_ref.shape[1:])


def _peer_scores(q, keys):
    t, n = q.shape
    h, _, nk, dk = keys.shape
    tt = _pick(t, (512, 256))
    big = jax.ShapeDtypeStruct((h, nk, t), F32)
    big_spec = pl.BlockSpec((1, nk, tt), lambda i, j: (j, 0, i))
    return pl.pallas_call(
        _peer_score_kernel,
        grid=(t // tt, h),
        in_specs=[pl.BlockSpec((tt, 2 * dk), lambda i, j: (i, j)),
                  pl.BlockSpec((1, 2, nk, dk), lambda i, j: (j, 0, 0, 0))],
        out_specs=[big_spec, big_spec, big_spec, big_spec,
                   pl.BlockSpec((1, 8, tt), lambda i, j: (j, 0, i))],
        out_shape=[big, big, big, big, jax.ShapeDtypeStruct((h, 8, t), F32)],
        compiler_params=_cparams("parallel", "parallel"),
        name="peer_scores",
    )(q, keys)


def _peer_expert_kernel(h_ref, u_ref, vt_ref, s1_ref, e1_ref, s2_ref, e2_ref, thr_ref,
                        o_ref, acc_ref, *, n_sub):
    j = pl.program_id(1)
    nk = s2_ref.shape[1]
    nh = s2_ref.shape[0]

    @pl.when(j == 0)
    def _():
        acc_ref[...] = jnp.zeros_like(acc_ref)

    act = lax.dot_general(u_ref[...], h_ref[...], NT_DIMS, preferred_element_type=F32)
    parts = []
    for sb in range(n_sub):
        i1 = j * n_sub + sb
        w = jnp.zeros((nk, act.shape[1]), F32)
        for hd in range(nh):
            s1row = s1_ref[hd, pl.ds(i1, 1), :]
            e1row = e1_ref[hd, pl.ds(i1, 1), :]
            sel = (s2_ref[hd] + s1row) >= thr_ref[hd, 0:1, :]
            w = w + jnp.where(sel, e2_ref[hd], 0.0) * e1row
        a = act[sb * nk:(sb + 1) * nk, :]
        gelu = 0.5 * a * (1.0 + lax.erf(a * (2.0 ** -0.5)))
        parts.append((gelu * w).astype(BF16))
    pmat = jnp.concatenate(parts, axis=0)
    acc_ref[...] += jnp.dot(vt_ref[...], pmat, preferred_element_type=F32)

    @pl.when(j == pl.num_programs(1) - 1)
    def _():
        o_ref[...] = acc_ref[...].T


def _peer_experts(h2, u, vt, s1, e1, s2, e2, thr):
    t, d = h2.shape
    e = u.shape[0]
    nh, nk, _ = s1.shape
    bm = _pick(t, (512, 256))
    eb = 512
    n_sub = eb // nk
    kern = functools.partial(_peer_expert_kernel, n_sub=n_sub)
    big_spec = pl.BlockSpec((nh, nk, bm), lambda i, j: (0, 0, i))
    return pl.pallas_call(
        kern,
        grid=(t // bm, e // eb),
        in_specs=[pl.BlockSpec((bm, d), lambda i, j: (i, 0)),
                  pl.BlockSpec((eb, d), lambda i, j: (j, 0)),
                  pl.BlockSpec((d, eb), lambda i, j: (0, j)),
                  big_spec, big_spec, big_spec, big_spec,
                  pl.BlockSpec((nh, 8, bm), lambda i, j: (0, 0, i))],
        out_specs=pl.BlockSpec((bm, d), lambda i, j: (i, 0)),
        out_shape=jax.ShapeDtypeStruct((t, d), F32),
        scratch_shapes=[pltpu.VMEM((d, bm), F32)],
        compiler_params=_cparams("parallel", "arbitrary"),
        name="peer_experts",
    )(h2, u, vt, s1, e1, s2, e2, thr)


def _final_kernel(x_ref, y_ref, g_ref, w_ref, o_ref):
    xf = x_ref[0] + g_ref[0] * y_ref[0]
    ms = jnp.mean(xf * xf, axis=-1, keepdims=True)
    o_ref[0] = xf * lax.rsqrt(ms + EPS) * w_ref[...]


def _final(x1, y, gate, w):
    b, l, d = x1.shape
    bm = _pick(l, (512, 256))
    return pl.pallas_call(
        _final_kernel,
        grid=(b, l // bm),
        in_specs=[pl.BlockSpec((1, bm, d), lambda i, t: (i, t, 0)),
                  pl.BlockSpec((1, bm, d), lambda i, t: (i, t, 0)),
                  pl.BlockSpec((1, 1, d), lambda i, t: (i, 0, 0)),
                  pl.BlockSpec((1, d), lambda i, t: (0, 0))],
        out_specs=pl.BlockSpec((1, bm, d), lambda i, t: (i, t, 0)),
        out_shape=jax.ShapeDtypeStruct((b, l, d), F32),
        compiler_params=_cparams("parallel", "parallel"),
        name="final_norm",
    )(x1, y, gate, w.reshape(1, d))


def _layer(x, ctx, mod_x, mod_c, p):
    b, l, d = x.shape
    lc = ctx.shape[1]
    s = l + lc
    dk = p['w_lr2_f'].shape[1]
    dv = p['w_gla_out'].shape[0]
    di = p['w_ssd_out'].shape[0]
    bc = SSD_GROUPS * SSD_STATE
    n_ssd_heads = p['a_log_f'].shape[0]

    sizes = (dk, dk, dv, dv, GLA_RANK, GLA_RANK, di, di + 2 * bc, n_ssd_heads, n_ssd_heads, 2 * d)
    offs = [0]
    for sz in sizes:
        offs.append(offs[-1] + sz)
    w_in = p['w_in']
    seg = lambda i: w_in[:, offs[i]:offs[i + 1]]
    w_main = jnp.concatenate([seg(0), seg(1), seg(2), seg(3), seg(6), seg(7), seg(10)], axis=1).astype(BF16)
    n_small = 2 * LANES
    w_small = jnp.concatenate(
        [seg(8), seg(9), seg(4), seg(5),
         jnp.zeros((d, n_small - 2 * n_ssd_heads - 2 * GLA_RANK), w_in.dtype)], axis=1).astype(BF16)
    q0, k0, v0, r0 = 0, dk, 2 * dk, 2 * dk + dv
    z0 = r0 + dv
    xbc0 = z0 + di
    gl0 = xbc0 + di + 2 * bc

    shift1 = jnp.stack([mod_x[0], jnp.broadcast_to(mod_c[0], (b, d))], axis=1).reshape(b, 2, 1, d)
    scale1 = jnp.stack([mod_x[1], jnp.broadcast_to(mod_c[1], (b, d))], axis=1).reshape(b, 2, 1, d)
    h = _norm_mod(x, ctx, p['norm1_w'], shift1, scale1)

    h2d = h.reshape(b * s, d)
    proj = _matmul(h2d, w_main, BF16, "in_proj").reshape(b, s, -1)
    small = _matmul(h2d, w_small, F32, "in_proj_gates").reshape(b, s, n_small)

    xbc = _conv(proj, xbc0 // 512, di + 2 * bc, p['conv_w'], p['conv_b'], l, lc)
    bias = jnp.concatenate([p['dt_bias_f'], p['dt_bias_b']]).reshape(1, LANES).astype(F32)
    a_neg = -jnp.exp(jnp.concatenate([p['a_log_f'], p['a_log_b']]).astype(F32)).reshape(1, LANES)
    dt, cum = _ssd_prep(small, bias, a_neg)
    g, hpg, c = SSD_GROUPS, SSD_HPG, SSD_CHUNK
    cumc = cum.reshape(b, s, 2, g, hpg).transpose(0, 3, 2, 1, 4).reshape(b * g, 2, s, hpg)
    to_rows = lambda t: t.reshape(b, s // c, c, 2, g, hpg).transpose(0, 4, 3, 1, 5, 2).reshape(
        b * g, 2, s // c, hpg, c)
    dskip = jnp.repeat(p['d_skip'].astype(F32), SSD_HEADDIM).reshape(g, 1, hpg * SSD_HEADDIM)
    ssd_nw = p['ssd_norm_w'].astype(F32).reshape(g, 1, hpg * SSD_HEADDIM)
    y_n = _ssd(xbc, proj, z0 // 512, cumc, to_rows(cum), to_rows(dt), dskip, ssd_nw, l, lc)

    a_n = _gla(proj, small, p['w_lr2_f'], p['b_lr_f'], p['w_lr2_b'], p['b_lr_b'], p['gla_norm_w'],
               l, lc, (q0, k0, v0, r0))

    m = _merge(a_n, y_n, p['w_gla_out'].astype(BF16), p['w_ssd_out'].astype(BF16), proj, gl0,
               p['b_gate'])
    return _resid(m, p['w_o'].astype(BF16), x, mod_x[2].reshape(b, 1, d))


def _peer(x1, mod_x, norm_w, wq, keys, u, v):
    b, l, d = x1.shape
    h2, q = _peer_q(x1, norm_w, mod_x[3].reshape(b, 1, d), mod_x[4].reshape(b, 1, d), wq.astype(BF16))
    s1, e1, s2, e2, thr = _peer_scores(q.reshape(b * l, -1), keys)
    y = _peer_experts(h2.reshape(b * l, d), u.astype(BF16), v.astype(BF16).T, s1, e1, s2, e2, thr)
    return y.reshape(b, l, d)


def kernel(x, c, ctx, c_ctx, w_ada, b_ada, norm1_w, w_in, b_gate, w_lr2_f, b_lr_f, w_lr2_b, b_lr_b,
           gla_norm_w, w_gla_out, conv_w, conv_b, a_log_f, a_log_b, dt_bias_f, dt_bias_b, d_skip,
           ssd_norm_w, w_ssd_out, w_o, norm2_w, peer_wq, peer_keys, peer_u, peer_v, final_norm_w):
    b, l, d = x.shape
    depth = w_in.shape[0]
    assert depth == 1, "context-stream update for deeper stacks is not implemented"
    layer = 0
    rows = -(-(b + 1) // 8) * 8
    c_all = jnp.concatenate([c, c_ctx[None, :], jnp.zeros((rows - b - 1, d), c.dtype)], axis=0)
    mod = _ada(c_all, w_ada[layer], b_ada[layer])
    mod_x = [mod[:b, i * d:(i + 1) * d] for i in range(N_MOD)]
    mod_c = [mod[b, i * d:(i + 1) * d] for i in range(N_MOD)]
    p = {
        'norm1_w': norm1_w[layer], 'w_in': w_in[layer], 'b_gate': b_gate[layer],
        'w_lr2_f': w_lr2_f[layer], 'b_lr_f': b_lr_f[layer],
        'w_lr2_b': w_lr2_b[layer], 'b_lr_b': b_lr_b[layer],
        'gla_norm_w': gla_norm_w[layer], 'w_gla_out': w_gla_out[layer],
        'conv_w': conv_w[layer], 'conv_b': conv_b[layer],
        'a_log_f': a_log_f[layer], 'a_log_b': a_log_b[layer],
        'dt_bias_f': dt_bias_f[layer], 'dt_bias_b': dt_bias_b[layer],
        'd_skip': d_skip[layer], 'ssd_norm_w': ssd_norm_w[layer],
        'w_ssd_out': w_ssd_out[layer], 'w_o': w_o[layer],
    }
    x1 = _layer(x, ctx, mod_x, mod_c, p)
    y = _peer(x1, mod_x, norm2_w[layer], peer_wq[layer], peer_keys[layer], peer_u[layer], peer_v[layer])
    return _final(x1, y, mod_x[5].reshape(b, 1, d), final_norm_w)
```

```python
import functools

import jax
import jax.numpy as jnp
from jax import lax
from jax.experimental import pallas as pl
from jax.experimental.pallas import tpu as pltpu

F32 = jnp.float32
BF16 = jnp.bfloat16
EPS = 1e-6
HIGHEST = lax.Precision.HIGHEST

N_MOD = 6
GRID_W = 64
GLA_HEADS = 4
GLA_RANK = 16
GLA_NORMALIZER = 16.0
GLA_CHUNK = 64
SSD_HEADDIM = 64
SSD_GROUPS = 8
SSD_HPG = 8
SSD_STATE = 128
SSD_CHUNK = 128
PEER_HEADS = 8
PEER_NKEYS = 128
PEER_TOPK = 16

VMEM_LIMIT_BYTES = 56 * 1024 * 1024
LANES = 128

NT_DIMS = (((1,), (1,)), ((), ()))
TN_DIMS = (((0,), (0,)), ((), ()))


def _cparams(*sem):
    return pltpu.CompilerParams(dimension_semantics=sem, vmem_limit_bytes=VMEM_LIMIT_BYTES)


def _pick(n, options):
    for o in options:
        if n % o == 0:
            return o
    raise ValueError(f"no tile in {options} divides {n}")


def _softplus(x):
    return jnp.maximum(x, 0.0) + jnp.log1p(jnp.exp(-jnp.abs(x)))


def _silu(x):
    return x * jax.nn.sigmoid(x)


def _ada_kernel(c_ref, w_ref, b_ref, o_ref):
    a = _silu(c_ref[...])
    o_ref[...] = jnp.dot(a, w_ref[...], precision=HIGHEST, preferred_element_type=F32) + b_ref[...]


def _ada(c_all, w_ada, b_ada):
    m, d = c_all.shape
    n = w_ada.shape[1]
    bn = _pick(n, (1024, 512, 256, 128))
    return pl.pallas_call(
        _ada_kernel,
        grid=(n // bn,),
        in_specs=[pl.BlockSpec((m, d), lambda j: (0, 0)),
                  pl.BlockSpec((d, bn), lambda j: (0, j)),
                  pl.BlockSpec((1, bn), lambda j: (0, j))],
        out_specs=pl.BlockSpec((m, bn), lambda j: (0, j)),
        out_shape=jax.ShapeDtypeStruct((m, n), F32),
        compiler_params=_cparams("parallel"),
        name="ada_mod",
    )(c_all, w_ada, b_ada.reshape(1, n))


def _norm_mod_kernel(x_ref, ctx_ref, w_ref, sh_ref, sc_ref, o_ref, *, n_lat_tiles):
    j = pl.program_id(1)

    def emit(src):
        xf = src[0]
        ms = jnp.mean(xf * xf, axis=-1, keepdims=True)
        y = xf * lax.rsqrt(ms + EPS) * w_ref[...]
        o_ref[0] = (y * (1.0 + sc_ref[0, 0]) + sh_ref[0, 0]).astype(o_ref.dtype)

    @pl.when(j < n_lat_tiles)
    def _():
        emit(x_ref)

    @pl.when(j >= n_lat_tiles)
    def _():
        emit(ctx_ref)


def _norm_mod(x, ctx, w, shift, scale):
    b, l, d = x.shape
    lc = ctx.shape[1]
    tn = _pick(lc, (256, 128))
    nl, nc = l // tn, lc // tn
    kern = functools.partial(_norm_mod_kernel, n_lat_tiles=nl)
    return pl.pallas_call(
        kern,
        grid=(b, nl + nc),
        in_specs=[pl.BlockSpec((1, tn, d), lambda i, j: (i, jnp.minimum(j, nl - 1), 0)),
                  pl.BlockSpec((1, tn, d), lambda i, j: (i, jnp.maximum(j - nl, 0), 0)),
                  pl.BlockSpec((1, d), lambda i, j: (0, 0)),
                  pl.BlockSpec((1, 1, 1, d), lambda i, j: (i, j // nl, 0, 0)),
                  pl.BlockSpec((1, 1, 1, d), lambda i, j: (i, j // nl, 0, 0))],
        out_specs=pl.BlockSpec((1, tn, d), lambda i, j: (i, j, 0)),
        out_shape=jax.ShapeDtypeStruct((b, l + lc, d), BF16),
        compiler_params=_cparams("parallel", "parallel"),
        name="norm_mod",
    )(x, ctx, w.reshape(1, d), shift, scale)


def _matmul_kernel(a_ref, b_ref, o_ref):
    o_ref[...] = jnp.dot(a_ref[...], b_ref[...], preferred_element_type=F32).astype(o_ref.dtype)


def _matmul(a, b, out_dtype, name):
    m, k = a.shape
    n = b.shape[1]
    bm = _pick(m, (1024, 512, 256))
    bn = _pick(n, (1024, 512, 256))
    return pl.pallas_call(
        _matmul_kernel,
        grid=(m // bm, n // bn),
        in_specs=[pl.BlockSpec((bm, k), lambda i, j: (i, 0)),
                  pl.BlockSpec((k, bn), lambda i, j: (0, j))],
        out_specs=pl.BlockSpec((bm, bn), lambda i, j: (i, j)),
        out_shape=jax.ShapeDtypeStruct((m, n), out_dtype),
        compiler_params=_cparams("parallel", "parallel"),
        name=name,
    )(a, b)


CONV_TOK = 256
CTX_HALO = 8
LAT_HALO = GRID_W + 8


def _conv_kernel(x_ref, w_ref, b_ref, o_ref, lat_ref, ctx_ref, *, l, lc):
    tc = x_ref.shape[2]
    lat_ref[0:LAT_HALO, :] = jnp.zeros((LAT_HALO, tc), F32)
    lat_ref[LAT_HALO + l:LAT_HALO + l + LAT_HALO, :] = jnp.zeros((LAT_HALO, tc), F32)
    ctx_ref[0:CTX_HALO, :] = jnp.zeros((CTX_HALO, tc), F32)
    ctx_ref[CTX_HALO + lc:CTX_HALO + lc + CTX_HALO, :] = jnp.zeros((CTX_HALO, tc), F32)
    lat_ref[LAT_HALO:LAT_HALO + l, :] = x_ref[0, 0:l, :].astype(F32)
    ctx_ref[CTX_HALO:CTX_HALO + lc, :] = x_ref[0, l:l + lc, :].astype(F32)

    bias = b_ref[...]
    col = lax.broadcasted_iota(jnp.int32, (CONV_TOK, tc), 0) % GRID_W
    not_first = col != 0
    not_last = col != GRID_W - 1

    for t0 in range(0, l, CONV_TOK):
        acc = jnp.zeros((CONV_TOK, tc), F32)
        for kw in range(3):
            part = jnp.zeros((CONV_TOK, tc), F32)
            for kh in range(3):
                start = LAT_HALO + t0 + GRID_W * (kh - 1) + (kw - 1)
                part = part + w_ref[kh * 3 + kw:kh * 3 + kw + 1, :] * lat_ref[start:start + CONV_TOK, :]
            if kw == 0:
                part = jnp.where(not_first, part, 0.0)
            elif kw == 2:
                part = jnp.where(not_last, part, 0.0)
            acc = acc + part
        o_ref[0, t0:t0 + CONV_TOK, :] = _silu(acc + bias).astype(o_ref.dtype)

    ctx_tok = min(CONV_TOK, lc)
    for t0 in range(0, lc, ctx_tok):
        acc = jnp.zeros((ctx_tok, tc), F32)
        for kw in range(3):
            start = CTX_HALO + t0 + (kw - 1)
            acc = acc + w_ref[3 + kw:4 + kw, :] * ctx_ref[start:start + ctx_tok, :]
        o_ref[0, l + t0:l + t0 + ctx_tok, :] = _silu(acc + bias).astype(o_ref.dtype)


def _conv(proj, col_block0, n_ch, conv_w, conv_b, l, lc):
    b, s, _ = proj.shape
    tc = 512
    kern = functools.partial(_conv_kernel, l=l, lc=lc)
    return pl.pallas_call(
        kern,
        grid=(b, n_ch // tc),
        in_specs=[pl.BlockSpec((1, s, tc), lambda i, j: (i, 0, col_block0 + j)),
                  pl.BlockSpec((9, tc), lambda i, j: (0, j)),
                  pl.BlockSpec((1, tc), lambda i, j: (0, j))],
        out_specs=pl.BlockSpec((1, s, tc), lambda i, j: (i, 0, j)),
        out_shape=jax.ShapeDtypeStruct((b, s, n_ch), BF16),
        scratch_shapes=[pltpu.VMEM((l + 2 * LAT_HALO, tc), F32),
                        pltpu.VMEM((lc + 2 * CTX_HALO, tc), F32)],
        compiler_params=_cparams("parallel", "parallel"),
        name="dwconv_silu",
    )(proj, conv_w.reshape(9, n_ch), conv_b.reshape(1, n_ch))


def _ssd_prep_kernel(x_ref, bias_ref, a_ref, dt_ref, cum_ref):
    c = SSD_CHUNK
    s = x_ref.shape[1]
    row = lax.broadcasted_iota(jnp.int32, (c, c), 0)
    colm = lax.broadcasted_iota(jnp.int32, (c, c), 1)
    tri_f = (row >= colm).astype(F32)
    tri_b = (row <= colm).astype(F32)
    is_fwd = lax.broadcasted_iota(jnp.int32, (c, LANES), 1) < (LANES // 2)
    for t0 in range(0, s, c):
        dt = _softplus(x_ref[0, t0:t0 + c, :] + bias_ref[...])
        dta = dt * a_ref[...]
        cf = jnp.dot(tri_f, dta, precision=HIGHEST, preferred_element_type=F32)
        cb = jnp.dot(tri_b, dta, precision=HIGHEST, preferred_element_type=F32)
        dt_ref[0, t0:t0 + c, :] = dt
        cum_ref[0, t0:t0 + c, :] = jnp.where(is_fwd, cf, cb)


def _ssd_prep(small, bias, a):
    b, s, _ = small.shape
    out = jax.ShapeDtypeStruct((b, s, LANES), F32)
    return pl.pallas_call(
        _ssd_prep_kernel,
        grid=(b,),
        in_specs=[pl.BlockSpec((1, s, LANES), lambda i: (i, 0, 0)),
                  pl.BlockSpec((1, LANES), lambda i: (0, 0)),
                  pl.BlockSpec((1, LANES), lambda i: (0, 0))],
        out_specs=[pl.BlockSpec((1, s, LANES), lambda i: (i, 0, 0)),
                   pl.BlockSpec((1, s, LANES), lambda i: (i, 0, 0))],
        out_shape=[out, out],
        compiler_params=_cparams("parallel"),
        name="ssd_prep",
    )(small, bias, a)


def _ssd_kernel(xs_ref, bm_ref, cm_ref, z_ref, cum3_ref, dt3_ref, cumr_ref, dtr_ref, e_seg_ref, e_ch_ref,
                dsk_ref, nw_ref, o_ref, s_ref, y_ref, *, l, lc):
    c = SSD_CHUNK
    p = SSD_HEADDIM
    nh = SSD_HPG
    n_lat, n_ctx = l // c, lc // c
    n_all = n_lat + n_ctx
    row = lax.broadcasted_iota(jnp.int32, (c, c), 0)
    colm = lax.broadcasted_iota(jnp.int32, (c, c), 1)
    masks = (row >= colm, row <= colm)

    def chunk(ci, d, emit):
        tok = pl.ds(pl.multiple_of(ci * c, c), c)
        last = c - 1 if d == 0 else 0
        x = xs_ref[0, tok, :]
        bmat = bm_ref[0, tok, :]
        cum3 = cum3_ref[0, d, tok, :]
        cum_ch = jnp.dot(cum3, e_ch_ref[...], preferred_element_type=F32)
        dt_ch = jnp.dot(dt3_ref[0, d, tok, :], e_ch_ref[...], preferred_element_type=F32)
        cum_end = cum_ch[last:last + 1, :]
        state = s_ref[d]
        if emit:
            cmat = cm_ref[0, tok, :]
            cumr = cumr_ref[0, d, ci]
            dtr = dtr_ref[0, d, ci]
            cum_seg = jnp.dot(cum3, e_seg_ref[...], preferred_element_type=F32)
            cb = lax.dot_general(cmat, bmat, NT_DIMS, preferred_element_type=F32)
            ys = []
            for h in range(nh):
                seg = cum_seg[:, h * c:(h + 1) * c] - cumr[h:h + 1, :]
                decay = jnp.exp(jnp.where(masks[d], seg, -jnp.inf))
                m = (cb * decay * dtr[h:h + 1, :]).astype(BF16)
                ys.append(jnp.dot(m, x[:, h * p:(h + 1) * p], preferred_element_type=F32))
            y = jnp.concatenate(ys, axis=1)
            y = y + jnp.dot(cmat, state.astype(BF16), preferred_element_type=F32) * jnp.exp(cum_ch)
            y_ref[tok, :] = y_ref[tok, :] + y
        xw = (x.astype(F32) * (jnp.exp(cum_end - cum_ch) * dt_ch)).astype(BF16)
        s_ref[d] = state * jnp.exp(cum_end) + lax.dot_general(bmat, xw, TN_DIMS,
                                                              preferred_element_type=F32)

    def ctx_step(i, carry):
        chunk(n_lat + i, 0, False)
        chunk(n_all - 1 - i, 1, False)
        return carry

    def lat_step(i, carry):
        chunk(i, 0, True)
        chunk(n_lat - 1 - i, 1, True)
        return carry

    rows = 256
    for t0 in range(0, l, rows):
        y_ref[t0:t0 + rows, :] = dsk_ref[0] * xs_ref[0, t0:t0 + rows, :].astype(F32)
    s_ref[...] = jnp.zeros_like(s_ref)
    lax.fori_loop(0, n_ctx, ctx_step, 0)
    lax.fori_loop(0, n_lat, lat_step, 0)

    rows = 256
    for t0 in range(0, l, rows):
        y = y_ref[t0:t0 + rows, :] * _silu(z_ref[0, t0:t0 + rows, :].astype(F32))
        ms = jnp.mean(y * y, axis=-1, keepdims=True)
        o_ref[0, t0:t0 + rows, :] = (y * lax.rsqrt(ms + EPS) * nw_ref[0]).astype(o_ref.dtype)


SPLIT_K = 32


def _split3_cols(t, b, s):
    g, hpg = SSD_GROUPS, SSD_HPG
    hi = t.astype(BF16)
    r1 = t - hi.astype(F32)
    mid = r1.astype(BF16)
    lo = (r1 - mid.astype(F32)).astype(BF16)
    parts = jnp.stack([hi, mid, lo, jnp.zeros_like(hi)], axis=2)
    parts = parts.reshape(b, s, SPLIT_K // hpg, 2, g, hpg).transpose(0, 4, 3, 1, 2, 5)
    return parts.reshape(b * g, 2, s, SPLIT_K)


def _expansion(width):
    k = jnp.arange(SPLIT_K)[:, None]
    n = jnp.arange(SSD_HPG * width)[None, :]
    return ((k % SSD_HPG == n // width) & (k < 3 * SSD_HPG)).astype(BF16)


def _ssd(xbc, proj, z_block0, cum3, dt3, cumr, dtr, dskip, norm_w, l, lc):
    b, s, _ = xbc.shape
    g = SSD_GROUPS
    gw = SSD_HPG * SSD_HEADDIM
    nchunks = s // SSD_CHUNK
    xs_blocks = g * gw // SSD_STATE
    e_seg = _expansion(SSD_CHUNK)
    e_ch = _expansion(SSD_HEADDIM)
    kern = functools.partial(_ssd_kernel, l=l, lc=lc)
    col_spec = pl.BlockSpec((1, 2, s, SPLIT_K), lambda i, j: (i * g + j, 0, 0, 0))
    row_spec = pl.BlockSpec((1, 2, nchunks, SSD_HPG, SSD_CHUNK), lambda i, j: (i * g + j, 0, 0, 0, 0))
    return pl.pallas_call(
        kern,
        grid=(b, g),
        in_specs=[pl.BlockSpec((1, s, gw), lambda i, j: (i, 0, j)),
                  pl.BlockSpec((1, s, SSD_STATE), lambda i, j: (i, 0, xs_blocks + j)),
                  pl.BlockSpec((1, s, SSD_STATE), lambda i, j: (i, 0, xs_blocks + g + j)),
                  pl.BlockSpec((1, l, gw), lambda i, j: (i, 0, z_block0 + j)),
                  col_spec, col_spec, row_spec, row_spec,
                  pl.BlockSpec(e_seg.shape, lambda i, j: (0, 0)),
                  pl.BlockSpec(e_ch.shape, lambda i, j: (0, 0)),
                  pl.BlockSpec((1, 1, gw), lambda i, j: (j, 0, 0)),
                  pl.BlockSpec((1, 1, gw), lambda i, j: (j, 0, 0))],
        out_specs=pl.BlockSpec((1, l, gw), lambda i, j: (i, 0, j)),
        out_shape=jax.ShapeDtypeStruct((b, l, g * gw), BF16),
        scratch_shapes=[pltpu.VMEM((2, SSD_STATE, gw), F32),
                        pltpu.VMEM((l, gw), F32)],
        compiler_params=_cparams("parallel", "parallel"),
        name="ssd_scan",
    )(xbc, xbc, xbc, proj, cum3, dt3, cumr, dtr, e_seg, e_ch, dskip, norm_w)


GLA_TILE = 256


def _split_dot(m, x):
    x_hi = x.astype(BF16)
    x_lo = (x - x_hi.astype(F32)).astype(BF16)
    return (jnp.dot(m, x_hi, preferred_element_type=F32) + jnp.dot(m, x_lo, preferred_element_type=F32))


def _gla_kernel(q_ref, k_ref, v_ref, r_ref, lr_ref, w2f_ref, b2f_ref, w2b_ref, b2b_ref, nw_ref,
                o_ref, st_ref, y_ref, qd_ref, kt_ref, eb_ref, *, l, lc):
    c = GLA_CHUNK
    t = GLA_TILE
    n_lat, n_ctx = l // c, lc // c
    n_all = n_lat + n_ctx
    hk = q_ref.shape[2]
    scale = hk ** -0.5
    row = lax.broadcasted_iota(jnp.int32, (t, t), 0)
    colm = lax.broadcasted_iota(jnp.int32, (t, t), 1)
    same_chunk = (row // c) == (colm // c)
    masks = (same_chunk & (row >= colm), same_chunk & (row <= colm))
    chunk_ones = same_chunk.astype(BF16)

    def gates(off, emit):
        tok = pl.ds(off, t)
        kf = k_ref[0, tok, :].astype(F32)
        lr_all = lr_ref[0, tok, :]
        for d in range(2):
            w2 = w2f_ref if d == 0 else w2b_ref
            b2 = b2f_ref if d == 0 else b2b_ref
            lr = lr_all[:, d * GLA_RANK:(d + 1) * GLA_RANK]
            logit = jnp.dot(lr, w2[...], precision=HIGHEST, preferred_element_type=F32) + b2[...]
            g = -_softplus(-logit) * (1.0 / GLA_NORMALIZER)
            bcum = _split_dot(masks[d].astype(BF16), g)
            btot = _split_dot(chunk_ones, g)
            kt_ref[d, tok, :] = (kf * jnp.exp(btot - bcum)).astype(BF16)
            ebt = jnp.exp(btot)
            for j in range(t // c):
                eb_ref[d, pl.ds(off // c + j, 1)] = ebt[j * c:j * c + 8, :][None]
            if emit:
                qd = (q_ref[0, tok, :].astype(F32) * scale * jnp.exp(bcum)).astype(BF16)
                kd = (kf * jnp.exp(-bcum)).astype(BF16)
                a = lax.dot_general(qd, kd, NT_DIMS, preferred_element_type=F32)
                a = jnp.where(masks[d], a, 0.0).astype(BF16)
                o = jnp.dot(a, v_ref[0, tok, :], preferred_element_type=F32)
                qd_ref[d, tok, :] = qd
                if d == 0:
                    y_ref[tok, :] = o
                else:
                    y_ref[tok, :] = y_ref[tok, :] + o

    def lat_gates(i, carry):
        gates(pl.multiple_of(i * t, t), True)
        return carry

    lax.fori_loop(0, l // t, lat_gates, 0)
    for i in range(lc // t):
        gates(l + i * t, False)

    def step(ci, d, emit):
        tok = pl.ds(pl.multiple_of(ci * c, c), c)
        st = st_ref[d]
        if emit:
            o = lax.dot_general(qd_ref[d, tok, :], st.astype(BF16), NT_DIMS, preferred_element_type=F32)
            y_ref[tok, :] = y_ref[tok, :] + o
        st_ref[d] = st * eb_ref[d, ci][0:1, :] + lax.dot_general(
            v_ref[0, tok, :], kt_ref[d, tok, :], TN_DIMS, preferred_element_type=F32)

    def ctx_step(i, carry):
        step(n_lat + i, 0, False)
        step(n_all - 1 - i, 1, False)
        return carry

    def lat_step(i, carry):
        step(i, 0, True)
        step(n_lat - 1 - i, 1, True)
        return carry

    st_ref[...] = jnp.zeros_like(st_ref)
    lax.fori_loop(0, n_ctx, ctx_step, 0)
    lax.fori_loop(0, n_lat, lat_step, 0)

    rows = 256
    for t0 in range(0, l, rows):
        o = y_ref[t0:t0 + rows, :]
        ms = jnp.mean(o * o, axis=-1, keepdims=True)
        o = o * lax.rsqrt(ms + EPS) * nw_ref[...]
        o_ref[0, t0:t0 + rows, :] = (o * _silu(r_ref[0, t0:t0 + rows, :].astype(F32))).astype(o_ref.dtype)


def _gla(proj, small, w2f, b2f, w2b, b2b, norm_w, l, lc, blocks):
    b, s, _ = proj.shape
    h = GLA_HEADS
    hk = w2f.shape[1] // h
    hv = norm_w.shape[0]
    q0, k0, v0, r0 = blocks
    kern = functools.partial(_gla_kernel, l=l, lc=lc)
    return pl.pallas_call(
        kern,
        grid=(b, h),
        in_specs=[pl.BlockSpec((1, s, hk), lambda i, j: (i, 0, q0 // hk + j)),
                  pl.BlockSpec((1, s, hk), lambda i, j: (i, 0, k0 // hk + j)),
                  pl.BlockSpec((1, s, hv), lambda i, j: (i, 0, v0 // hv + j)),
                  pl.BlockSpec((1, l, hv), lambda i, j: (i, 0, r0 // hv + j)),
                  pl.BlockSpec((1, s, LANES), lambda i, j: (i, 0, 1)),
                  pl.BlockSpec((GLA_RANK, hk), lambda i, j: (0, j)),
                  pl.BlockSpec((1, hk), lambda i, j: (0, j)),
                  pl.BlockSpec((GLA_RANK, hk), lambda i, j: (0, j)),
                  pl.BlockSpec((1, hk), lambda i, j: (0, j)),
                  pl.BlockSpec((1, hv), lambda i, j: (0, 0))],
        out_specs=pl.BlockSpec((1, l, hv), lambda i, j: (i, 0, j)),
        out_shape=jax.ShapeDtypeStruct((b, l, h * hv), BF16),
        scratch_shapes=[pltpu.VMEM((2, hv, hk), F32),
                        pltpu.VMEM((l, hv), F32),
                        pltpu.VMEM((2, l, hk), BF16),
                        pltpu.VMEM((2, s, hk), BF16),
                        pltpu.VMEM((2, s // GLA_CHUNK, 8, hk), F32)],
        compiler_params=_cparams("parallel", "parallel"),
        name="gla_scan",
    )(proj, proj, proj, proj, small, w2f, b2f.reshape(1, -1), w2b, b2b.reshape(1, -1),
      norm_w.reshape(1, hv))


def _merge_kernel(a_ref, y_ref, wa_ref, wb_ref, gla_ref, glb_ref, ba_ref, bb_ref, o_ref):
    ya = jnp.dot(a_ref[0], wa_ref[...], preferred_element_type=F32)
    yb = jnp.dot(y_ref[0], wb_ref[...], preferred_element_type=F32)
    ga = jax.nn.sigmoid(gla_ref[0].astype(F32) + ba_ref[...])
    gb = jax.nn.sigmoid(glb_ref[0].astype(F32) + bb_ref[...])
    o_ref[0] = (ga * ya + gb * yb).astype(o_ref.dtype)


def _merge(a_n, y_n, w_a, w_b, proj, gl0, b_gate):
    b, l, ka = a_n.shape
    kb = y_n.shape[2]
    d = w_a.shape[1]
    bm = _pick(l, (512, 256))
    bn = 512
    nb = d // bn
    bg = b_gate.reshape(1, 2 * d)
    return pl.pallas_call(
        _merge_kernel,
        grid=(b, l // bm, nb),
        in_specs=[pl.BlockSpec((1, bm, ka), lambda i, t, j: (i, t, 0)),
                  pl.BlockSpec((1, bm, kb), lambda i, t, j: (i, t, 0)),
                  pl.BlockSpec((ka, bn), lambda i, t, j: (0, j)),
                  pl.BlockSpec((kb, bn), lambda i, t, j: (0, j)),
                  pl.BlockSpec((1, bm, bn), lambda i, t, j: (i, t, gl0 // bn + j)),
                  pl.BlockSpec((1, bm, bn), lambda i, t, j: (i, t, gl0 // bn + nb + j)),
                  pl.BlockSpec((1, bn), lambda i, t, j: (0, j)),
                  pl.BlockSpec((1, bn), lambda i, t, j: (0, nb + j))],
        out_specs=pl.BlockSpec((1, bm, bn), lambda i, t, j: (i, t, j)),
        out_shape=jax.ShapeDtypeStruct((b, l, d), BF16),
        compiler_params=_cparams("parallel", "parallel", "parallel"),
        name="branch_merge",
    )(a_n, y_n, w_a, w_b, proj, proj, bg, bg)


def _resid_kernel(m_ref, w_ref, x_ref, g_ref, o_ref):
    y = jnp.dot(m_ref[0], w_ref[...], preferred_element_type=F32)
    o_ref[0] = x_ref[0] + g_ref[0] * y


def _resid(m, w_o, x, gate):
    b, l, d = x.shape
    bm = _pick(l, (1024, 512, 256))
    bn = 512
    return pl.pallas_call(
        _resid_kernel,
        grid=(b, l // bm, d // bn),
        in_specs=[pl.BlockSpec((1, bm, d), lambda i, t, j: (i, t, 0)),
                  pl.BlockSpec((d, bn), lambda i, t, j: (0, j)),
                  pl.BlockSpec((1, bm, bn), lambda i, t, j: (i, t, j)),
                  pl.BlockSpec((1, 1, bn), lambda i, t, j: (i, 0, j))],
        out_specs=pl.BlockSpec((1, bm, bn), lambda i, t, j: (i, t, j)),
        out_shape=jax.ShapeDtypeStruct((b, l, d), F32),
        compiler_params=_cparams("parallel", "parallel", "parallel"),
        name="attn_resid",
    )(m, w_o, x, gate)


def _peer_q_kernel(x_ref, nw_ref, sh_ref, sc_ref, wq_ref, h_ref, q_ref, lhs_ref):
    @pl.when(pl.program_id(2) == 0)
    def _():
        xf = x_ref[0]
        ms = jnp.mean(xf * xf, axis=-1, keepdims=True)
        y = xf * lax.rsqrt(ms + EPS) * nw_ref[...]
        hb = (y * (1.0 + sc_ref[0]) + sh_ref[0]).astype(BF16)
        lhs_ref[...] = hb
        h_ref[0] = hb

    q_ref[0] = jnp.dot(lhs_ref[...], wq_ref[...], preferred_element_type=F32)


def _peer_q(x1, norm_w, shift, scale, wq):
    b, l, d = x1.shape
    n = wq.shape[1]
    bm = _pick(l, (512, 256))
    bn = n
    return pl.pallas_call(
        _peer_q_kernel,
        grid=(b, l // bm, n // bn),
        in_specs=[pl.BlockSpec((1, bm, d), lambda i, t, j: (i, t, 0)),
                  pl.BlockSpec((1, d), lambda i, t, j: (0, 0)),
                  pl.BlockSpec((1, 1, d), lambda i, t, j: (i, 0, 0)),
                  pl.BlockSpec((1, 1, d), lambda i, t, j: (i, 0, 0)),
                  pl.BlockSpec((d, bn), lambda i, t, j: (0, j))],
        out_specs=[pl.BlockSpec((1, bm, d), lambda i, t, j: (i, t, 0)),
                   pl.BlockSpec((1, bm, bn), lambda i, t, j: (i, t, j))],
        out_shape=[jax.ShapeDtypeStruct((b, l, d), BF16),
                   jax.ShapeDtypeStruct((b, l, n), F32)],
        scratch_shapes=[pltpu.VMEM((bm, d), BF16)],
        compiler_params=_cparams("parallel", "parallel", "arbitrary"),
        name="peer_query",
    )(x1, norm_w.reshape(1, d), shift, scale, wq)


def _top_values(x, k, rows):
    sub = lax.broadcasted_iota(jnp.int32, (rows, x.shape[1]), 0)
    out = jnp.full((rows, x.shape[1]), -jnp.inf, F32)
    for i in range(k):
        m = jnp.max(x, axis=0, keepdims=True)
        out = jnp.where(sub == i, m, out)
        x = jnp.where(x == m, -jnp.inf, x)
    return out


def _peer_score_kernel(q_ref, keys_ref, c1_ref, e1_ref, s2_ref, e2_ref):
    k = PEER_TOPK
    dk = keys_ref.shape[3]
    q = q_ref[...]
    s1 = lax.dot_general(keys_ref[0, 0], q[:, 0:dk], NT_DIMS, precision=HIGHEST,
                         preferred_element_type=F32)
    s2 = lax.dot_general(keys_ref[0, 1], q[:, dk:2 * dk], NT_DIMS, precision=HIGHEST,
                         preferred_element_type=F32)
    n = k + 1
    pad = -(-n // 8) * 8
    sv1 = _top_values(s1, n, pad)
    sv2 = _top_values(s2, n, pad)
    assert n // 2 <= 8 and n // 9 == 1
    cand = jnp.concatenate(
        [sv1[0:1, :] + sv2]
        + [sv1[a:a + 1, :] + sv2[0:8, :] for a in range(1, 8)]
        + [sv1[8:pad, :] + sv2[0:1, :]], axis=0)
    cv = _top_values(cand, n, pad)
    z = jnp.sum(jnp.exp(cv[0:k, :] - cv[0:1, :]), axis=0, keepdims=True)
    thr = 0.5 * (cv[k - 1:k, :] + cv[k:k + 1, :])
    c1_ref[0] = thr - s1
    s2_ref[0] = s2
    e1_ref[0] = jnp.exp(s1 - sv1[0:1, :]) / z
    e2_ref[0] = jnp.exp(s2 - sv2[0:1, :])


def _peer_scores(q, keys):
    t, n = q.shape
    h, _, nk, dk = keys.shape
    tt = _pick(t, (512, 256))
    big = jax.ShapeDtypeStruct((h, nk, t), F32)
    big_spec = pl.BlockSpec((1, nk, tt), lambda i, j: (j, 0, i))
    return pl.pallas_call(
        _peer_score_kernel,
        grid=(t // tt, h),
        in_specs=[pl.BlockSpec((tt, 2 * dk), lambda i, j: (i, j)),
                  pl.BlockSpec((1, 2, nk, dk), lambda i, j: (j, 0, 0, 0))],
        out_specs=[big_spec, big_spec, big_spec, big_spec],
        out_shape=[big, big, big, big],
        compiler_params=_cparams("parallel", "parallel"),
        name="peer_scores",
    )(q, keys)


PEER_EXPERT_BLOCK = 1024
GATE_ROWS = 16


def _peer_expert_kernel(h_ref, u_ref, vt_ref, c1_ref, e1_ref, s2_ref, e2_ref,
                        o_ref, ht_ref, w_ref, c8_ref, e8_ref, *, n_sub):
    j = pl.program_id(1)
    nh, nk, bm = s2_ref.shape
    sub = 8

    @pl.when(j == 0)
    def _():
        o_ref[...] = jnp.zeros_like(o_ref)
        ht_ref[...] = h_ref[...].astype(F32).T.astype(BF16)

    i1_0 = pl.multiple_of(j * n_sub, n_sub)
    for hd in range(nh):
        cgrp = c1_ref[hd, pl.ds(i1_0, n_sub), :]
        egrp = e1_ref[hd, pl.ds(i1_0, n_sub), :]
        for sb in range(n_sub):
            r0 = (hd * n_sub + sb) * sub
            c8_ref[r0:r0 + sub, :] = jnp.broadcast_to(cgrp[sb:sb + 1, :], (sub, bm))
            e8_ref[r0:r0 + sub, :] = jnp.broadcast_to(egrp[sb:sb + 1, :], (sub, bm))

    act = jnp.dot(u_ref[...], ht_ref[...], preferred_element_type=F32)

    n_part = GATE_ROWS // sub
    for lc in range(bm // LANES):
        ls = slice(lc * LANES, (lc + 1) * LANES)
        for rc in range(nk // GATE_ROWS):
            accs = [[jnp.zeros((sub, LANES), F32) for _ in range(n_part)] for _ in range(n_sub)]
            for hd in range(nh):
                s2v = [s2_ref[hd, rc * GATE_ROWS + k * sub:rc * GATE_ROWS + (k + 1) * sub, ls]
                       for k in range(n_part)]
                e2v = [e2_ref[hd, rc * GATE_ROWS + k * sub:rc * GATE_ROWS + (k + 1) * sub, ls]
                       for k in range(n_part)]
                for sb in range(n_sub):
                    r0 = (hd * n_sub + sb) * sub
                    cut = c8_ref[r0:r0 + sub, ls]
                    e1v = e8_ref[r0:r0 + sub, ls]
                    for k in range(n_part):
                        accs[sb][k] = accs[sb][k] + jnp.where(s2v[k] >= cut, e2v[k], 0.0) * e1v
            for sb in range(n_sub):
                for k in range(n_part):
                    r0 = sb * nk + rc * GATE_ROWS + k * sub
                    w_ref[r0:r0 + sub, ls] = accs[sb][k]

    gelu =0.5 * act * (1.0 + lax.erf(act * (2.0 ** -0.5)))
    pmat = (gelu * w_ref[...]).astype(BF16)
    o_ref[...] += jnp.dot(vt_ref[...], pmat, preferred_element_type=F32)


def _peer_experts(h2, u, vt, c1, e1, s2, e2):
    t, d = h2.shape
    e = u.shape[0]
    nh, nk, _ = s2.shape
    bm = _pick(t, (512, 256))
    eb = PEER_EXPERT_BLOCK
    n_sub = eb // nk
    kern = functools.partial(_peer_expert_kernel, n_sub=n_sub)
    big_spec = pl.BlockSpec((nh, nk, bm), lambda i, j: (0, 0, i))
    return pl.pallas_call(
        kern,
        grid=(t // bm, e // eb),
        in_specs=[pl.BlockSpec((bm, d), lambda i, j: (i, 0)),
                  pl.BlockSpec((eb, d), lambda i, j: (j, 0)),
                  pl.BlockSpec((d, eb), lambda i, j: (0, j)),
                  big_spec, big_spec, big_spec, big_spec],
        out_specs=pl.BlockSpec((d, bm), lambda i, j: (0, i)),
        out_shape=jax.ShapeDtypeStruct((d, t), F32),
        scratch_shapes=[pltpu.VMEM((d, bm), BF16),
                        pltpu.VMEM((eb, bm), F32),
                        pltpu.VMEM((nh * n_sub * 8, bm), F32),
                        pltpu.VMEM((nh * n_sub * 8, bm), F32)],
        compiler_params=_cparams("parallel", "arbitrary"),
        name="peer_experts",
    )(h2, u, vt, c1, e1, s2, e2)


def _final_kernel(x_ref, yt_ref, g_ref, w_ref, o_ref):
    xf = x_ref[0] + g_ref[0] * yt_ref[...].T
    ms = jnp.mean(xf * xf, axis=-1, keepdims=True)
    o_ref[0] = xf * lax.rsqrt(ms + EPS) * w_ref[...]


def _final(x1, yt, gate, w):
    b, l, d = x1.shape
    bm = _pick(l, (512, 256))
    nt = l // bm
    return pl.pallas_call(
        _final_kernel,
        grid=(b, l // bm),
        in_specs=[pl.BlockSpec((1, bm, d), lambda i, t: (i, t, 0)),
                  pl.BlockSpec((d, bm), lambda i, t: (0, i * nt + t)),
                  pl.BlockSpec((1, 1, d), lambda i, t: (i, 0, 0)),
                  pl.BlockSpec((1, d), lambda i, t: (0, 0))],
        out_specs=pl.BlockSpec((1, bm, d), lambda i, t: (i, t, 0)),
        out_shape=jax.ShapeDtypeStruct((b, l, d), F32),
        compiler_params=_cparams("parallel", "parallel"),
        name="final_norm",
    )(x1, yt, gate, w.reshape(1, d))


def _layer(x, ctx, mod_x, mod_c, p):
    b, l, d = x.shape
    lc = ctx.shape[1]
    s = l + lc
    dk = p['w_lr2_f'].shape[1]
    dv = p['w_gla_out'].shape[0]
    di = p['w_ssd_out'].shape[0]
    bc = SSD_GROUPS * SSD_STATE
    n_ssd_heads = p['a_log_f'].shape[0]

    sizes = (dk, dk, dv, dv, GLA_RANK, GLA_RANK, di, di + 2 * bc, n_ssd_heads, n_ssd_heads, 2 * d)
    offs = [0]
    for sz in sizes:
        offs.append(offs[-1] + sz)
    w_in = p['w_in']
    seg = lambda i: w_in[:, offs[i]:offs[i + 1]]
    w_main = jnp.concatenate([seg(0), seg(1), seg(2), seg(3), seg(6), seg(7), seg(10)], axis=1).astype(BF16)
    n_small = 2 * LANES
    w_small = jnp.concatenate(
        [seg(8), seg(9), seg(4), seg(5),
         jnp.zeros((d, n_small - 2 * n_ssd_heads - 2 * GLA_RANK), w_in.dtype)], axis=1).astype(BF16)
    q0, k0, v0, r0 = 0, dk, 2 * dk, 2 * dk + dv
    z0 = r0 + dv
    xbc0 = z0 + di
    gl0 = xbc0 + di + 2 * bc

    shift1 = jnp.stack([mod_x[0], jnp.broadcast_to(mod_c[0], (b, d))], axis=1).reshape(b, 2, 1, d)
    scale1 = jnp.stack([mod_x[1], jnp.broadcast_to(mod_c[1], (b, d))], axis=1).reshape(b, 2, 1, d)
    h = _norm_mod(x, ctx, p['norm1_w'], shift1, scale1)

    h2d = h.reshape(b * s, d)
    proj = _matmul(h2d, w_main, BF16, "in_proj").reshape(b, s, -1)
    small = _matmul(h2d, w_small, F32, "in_proj_gates").reshape(b, s, n_small)

    xbc = _conv(proj, xbc0 // 512, di + 2 * bc, p['conv_w'], p['conv_b'], l, lc)
    bias = jnp.concatenate([p['dt_bias_f'], p['dt_bias_b']]).reshape(1, LANES).astype(F32)
    a_neg = -jnp.exp(jnp.concatenate([p['a_log_f'], p['a_log_b']]).astype(F32)).reshape(1, LANES)
    dt, cum = _ssd_prep(small, bias, a_neg)
    g, hpg, c = SSD_GROUPS, SSD_HPG, SSD_CHUNK
    to_rows = lambda t: t.reshape(b, s // c, c, 2, g, hpg).transpose(0, 4, 3, 1, 5, 2).reshape(
        b * g, 2, s // c, hpg, c)
    dskip = jnp.repeat(p['d_skip'].astype(F32), SSD_HEADDIM).reshape(g, 1, hpg * SSD_HEADDIM)
    ssd_nw = p['ssd_norm_w'].astype(F32).reshape(g, 1, hpg * SSD_HEADDIM)
    y_n = _ssd(xbc, proj, z0 // 512, _split3_cols(cum, b, s), _split3_cols(dt, b, s),
               to_rows(cum), to_rows(dt), dskip, ssd_nw, l, lc)

    a_n = _gla(proj, small, p['w_lr2_f'], p['b_lr_f'], p['w_lr2_b'], p['b_lr_b'], p['gla_norm_w'],
               l, lc, (q0, k0, v0, r0))

    m = _merge(a_n, y_n, p['w_gla_out'].astype(BF16), p['w_ssd_out'].astype(BF16), proj, gl0,
               p['b_gate'])
    return _resid(m, p['w_o'].astype(BF16), x, mod_x[2].reshape(b, 1, d))


def _peer(x1, mod_x, norm_w, wq, keys, u, v):
    b, l, d = x1.shape
    h2, q = _peer_q(x1, norm_w, mod_x[3].reshape(b, 1, d), mod_x[4].reshape(b, 1, d), wq.astype(BF16))
    c1, e1, s2, e2 = _peer_scores(q.reshape(b * l, -1), keys)
    return _peer_experts(h2.reshape(b * l, d), u.astype(BF16), v.astype(BF16).T, c1, e1, s2, e2)


def kernel(x, c, ctx, c_ctx, w_ada, b_ada, norm1_w, w_in, b_gate, w_lr2_f, b_lr_f, w_lr2_b, b_lr_b,
           gla_norm_w, w_gla_out, conv_w, conv_b, a_log_f, a_log_b, dt_bias_f, dt_bias_b, d_skip,
           ssd_norm_w, w_ssd_out, w_o, norm2_w, peer_wq, peer_keys, peer_u, peer_v, final_norm_w):
    b, l, d = x.shape
    depth = w_in.shape[0]
    assert depth == 1, "context-stream update for deeper stacks is not implemented"
    layer = 0
    rows = -(-(b + 1) // 8) * 8
    c_all = jnp.concatenate([c, c_ctx[None, :], jnp.zeros((rows - b - 1, d), c.dtype)], axis=0)
    mod = _ada(c_all, w_ada[layer], b_ada[layer])
    mod_x = [mod[:b, i * d:(i + 1) * d] for i in range(N_MOD)]
    mod_c = [mod[b, i * d:(i + 1) * d] for i in range(N_MOD)]
    p = {
        'norm1_w': norm1_w[layer], 'w_in': w_in[layer], 'b_gate': b_gate[layer],
        'w_lr2_f': w_lr2_f[layer], 'b_lr_f': b_lr_f[layer],
        'w_lr2_b': w_lr2_b[layer], 'b_lr_b': b_lr_b[layer],
        'gla_norm_w': gla_norm_w[layer], 'w_gla_out': w_gla_out[layer],
        'conv_w': conv_w[layer], 'conv_b': conv_b[layer],
        'a_log_f': a_log_f[layer], 'a_log_b': a_log_b[layer],
        'dt_bias_f': dt_bias_f[layer], 'dt_bias_b': dt_bias_b[layer],
        'd_skip': d_skip[layer], 'ssd_norm_w': ssd_norm_w[layer],
        'w_ssd_out': w_ssd_out[layer], 'w_o': w_o[layer],
    }
    x1 = _layer(x, ctx, mod_x, mod_c, p)
    y = _peer(x1, mod_x, norm2_w[layer], peer_wq[layer], peer_keys[layer], peer_u[layer], peer_v[layer])
    return _final(x1, y, mod_x[5].reshape(b, 1, d), final_norm_w)
```

```python
import functools

import jax
import jax.numpy as jnp
from jax import lax
from jax.experimental import pallas as pl
from jax.experimental.pallas import tpu as pltpu

F32 = jnp.float32
BF16 = jnp.bfloat16
EPS = 1e-6
HIGHEST = lax.Precision.HIGHEST

N_MOD = 6
GRID_W = 64
GLA_HEADS = 4
GLA_RANK = 16
GLA_NORMALIZER = 16.0
GLA_CHUNK = 64
SSD_HEADDIM = 64
SSD_GROUPS = 8
SSD_HPG = 8
SSD_STATE = 128
SSD_CHUNK = 128
PEER_HEADS = 8
PEER_NKEYS = 128
PEER_TOPK = 16

VMEM_LIMIT_BYTES = 56 * 1024 * 1024
LANES = 128

NT_DIMS = (((1,), (1,)), ((), ()))
TN_DIMS = (((0,), (0,)), ((), ()))


def _cparams(*sem, flags=None):
    return pltpu.CompilerParams(dimension_semantics=sem, vmem_limit_bytes=VMEM_LIMIT_BYTES, flags=flags)


def _pick(n, options):
    for o in options:
        if n % o == 0:
            return o
    raise ValueError(f"no tile in {options} divides {n}")


def _softplus(x):
    return jnp.maximum(x, 0.0) + jnp.log1p(jnp.exp(-jnp.abs(x)))


def _silu(x):
    return x * jax.nn.sigmoid(x)


def _ada_kernel(c_ref, w_ref, b_ref, o_ref):
    a = _silu(c_ref[...])
    o_ref[...] = jnp.dot(a, w_ref[...], precision=HIGHEST, preferred_element_type=F32) + b_ref[...]


def _ada(c_all, w_ada, b_ada):
    m, d = c_all.shape
    n = w_ada.shape[1]
    bn = _pick(n, (1024, 512, 256, 128))
    return pl.pallas_call(
        _ada_kernel,
        grid=(n // bn,),
        in_specs=[pl.BlockSpec((m, d), lambda j: (0, 0)),
                  pl.BlockSpec((d, bn), lambda j: (0, j)),
                  pl.BlockSpec((1, bn), lambda j: (0, j))],
        out_specs=pl.BlockSpec((m, bn), lambda j: (0, j)),
        out_shape=jax.ShapeDtypeStruct((m, n), F32),
        compiler_params=_cparams("parallel"),
        name="ada_mod",
    )(c_all, w_ada, b_ada.reshape(1, n))


def _norm_mod_kernel(x_ref, ctx_ref, w_ref, sh_ref, sc_ref, o_ref, *, n_lat_tiles):
    j = pl.program_id(1)

    def emit(src):
        xf = src[0]
        ms = jnp.mean(xf * xf, axis=-1, keepdims=True)
        y = xf * lax.rsqrt(ms + EPS) * w_ref[...]
        o_ref[0] = (y * (1.0 + sc_ref[0, 0]) + sh_ref[0, 0]).astype(o_ref.dtype)

    @pl.when(j < n_lat_tiles)
    def _():
        emit(x_ref)

    @pl.when(j >= n_lat_tiles)
    def _():
        emit(ctx_ref)


def _norm_mod(x, ctx, w, shift, scale):
    b, l, d = x.shape
    lc = ctx.shape[1]
    tn = _pick(lc, (256, 128))
    nl, nc = l // tn, lc // tn
    kern = functools.partial(_norm_mod_kernel, n_lat_tiles=nl)
    return pl.pallas_call(
        kern,
        grid=(b, nl + nc),
        in_specs=[pl.BlockSpec((1, tn, d), lambda i, j: (i, jnp.minimum(j, nl - 1), 0)),
                  pl.BlockSpec((1, tn, d), lambda i, j: (i, jnp.maximum(j - nl, 0), 0)),
                  pl.BlockSpec((1, d), lambda i, j: (0, 0)),
                  pl.BlockSpec((1, 1, 1, d), lambda i, j: (i, j // nl, 0, 0)),
                  pl.BlockSpec((1, 1, 1, d), lambda i, j: (i, j // nl, 0, 0))],
        out_specs=pl.BlockSpec((1, tn, d), lambda i, j: (i, j, 0)),
        out_shape=jax.ShapeDtypeStruct((b, l + lc, d), BF16),
        compiler_params=_cparams("parallel", "parallel"),
        name="norm_mod",
    )(x, ctx, w.reshape(1, d), shift, scale)


def _matmul_kernel(a_ref, b_ref, o_ref):
    o_ref[...] = jnp.dot(a_ref[...], b_ref[...], preferred_element_type=F32).astype(o_ref.dtype)


def _matmul(a, b, out_dtype, name):
    m, k = a.shape
    n = b.shape[1]
    bm = _pick(m, (1024, 512, 256))
    bn = _pick(n, (1024, 512, 256))
    return pl.pallas_call(
        _matmul_kernel,
        grid=(m // bm, n // bn),
        in_specs=[pl.BlockSpec((bm, k), lambda i, j: (i, 0)),
                  pl.BlockSpec((k, bn), lambda i, j: (0, j))],
        out_specs=pl.BlockSpec((bm, bn), lambda i, j: (i, j)),
        out_shape=jax.ShapeDtypeStruct((m, n), out_dtype),
        compiler_params=_cparams("parallel", "parallel"),
        name=name,
    )(a, b)


CONV_TOK = 256
CTX_HALO = 8
LAT_HALO = GRID_W + 8


def _conv_kernel(x_ref, w_ref, b_ref, o_ref, lat_ref, ctx_ref, *, l, lc):
    tc = x_ref.shape[2]
    lat_ref[0:LAT_HALO, :] = jnp.zeros((LAT_HALO, tc), F32)
    lat_ref[LAT_HALO + l:LAT_HALO + l + LAT_HALO, :] = jnp.zeros((LAT_HALO, tc), F32)
    ctx_ref[0:CTX_HALO, :] = jnp.zeros((CTX_HALO, tc), F32)
    ctx_ref[CTX_HALO + lc:CTX_HALO + lc + CTX_HALO, :] = jnp.zeros((CTX_HALO, tc), F32)
    lat_ref[LAT_HALO:LAT_HALO + l, :] = x_ref[0, 0:l, :].astype(F32)
    ctx_ref[CTX_HALO:CTX_HALO + lc, :] = x_ref[0, l:l + lc, :].astype(F32)

    bias = b_ref[...]
    col = lax.broadcasted_iota(jnp.int32, (CONV_TOK, tc), 0) % GRID_W
    not_first = col != 0
    not_last = col != GRID_W - 1

    for t0 in range(0, l, CONV_TOK):
        acc = jnp.zeros((CONV_TOK, tc), F32)
        for kw in range(3):
            part = jnp.zeros((CONV_TOK, tc), F32)
            for kh in range(3):
                start = LAT_HALO + t0 + GRID_W * (kh - 1) + (kw - 1)
                part = part + w_ref[kh * 3 + kw:kh * 3 + kw + 1, :] * lat_ref[start:start + CONV_TOK, :]
            if kw == 0:
                part = jnp.where(not_first, part, 0.0)
            elif kw == 2:
                part = jnp.where(not_last, part, 0.0)
            acc = acc + part
        o_ref[0, t0:t0 + CONV_TOK, :] = _silu(acc + bias).astype(o_ref.dtype)

    ctx_tok = min(CONV_TOK, lc)
    for t0 in range(0, lc, ctx_tok):
        acc = jnp.zeros((ctx_tok, tc), F32)
        for kw in range(3):
            start = CTX_HALO + t0 + (kw - 1)
            acc = acc + w_ref[3 + kw:4 + kw, :] * ctx_ref[start:start + ctx_tok, :]
        o_ref[0, l + t0:l + t0 + ctx_tok, :] = _silu(acc + bias).astype(o_ref.dtype)


def _conv(proj, col_block0, n_ch, conv_w, conv_b, l, lc):
    b, s, _ = proj.shape
    tc = 512
    kern = functools.partial(_conv_kernel, l=l, lc=lc)
    return pl.pallas_call(
        kern,
        grid=(b, n_ch // tc),
        in_specs=[pl.BlockSpec((1, s, tc), lambda i, j: (i, 0, col_block0 + j)),
                  pl.BlockSpec((9, tc), lambda i, j: (0, j)),
                  pl.BlockSpec((1, tc), lambda i, j: (0, j))],
        out_specs=pl.BlockSpec((1, s, tc), lambda i, j: (i, 0, j)),
        out_shape=jax.ShapeDtypeStruct((b, s, n_ch), BF16),
        scratch_shapes=[pltpu.VMEM((l + 2 * LAT_HALO, tc), F32),
                        pltpu.VMEM((lc + 2 * CTX_HALO, tc), F32)],
        compiler_params=_cparams("parallel", "parallel"),
        name="dwconv_silu",
    )(proj, conv_w.reshape(9, n_ch), conv_b.reshape(1, n_ch))


def _bf16_terms(x):
    hi = x.astype(BF16)
    r1 = x - hi.astype(F32)
    mid = r1.astype(BF16)
    lo = (r1 - mid.astype(F32)).astype(BF16)
    return hi, mid, lo


def _ssd_prep_kernel(x_ref, bias_ref, a_ref, dt_ref, cum_ref, dt3_ref, cum3_ref):
    c = SSD_CHUNK
    s = x_ref.shape[1]
    row = lax.broadcasted_iota(jnp.int32, (c, c), 0)
    colm = lax.broadcasted_iota(jnp.int32, (c, c), 1)
    tri_f = (row >= colm).astype(F32)
    tri_b = (row <= colm).astype(F32)
    is_fwd = lax.broadcasted_iota(jnp.int32, (c, LANES), 1) < (LANES // 2)
    for t0 in range(0, s, c):
        dt = _softplus(x_ref[0, t0:t0 + c, :] + bias_ref[...])
        dta = dt * a_ref[...]
        cf = jnp.dot(tri_f, dta, precision=HIGHEST, preferred_element_type=F32)
        cb = jnp.dot(tri_b, dta, precision=HIGHEST, preferred_element_type=F32)
        cum = jnp.where(is_fwd, cf, cb)
        dt_ref[0, t0:t0 + c, :] = dt
        cum_ref[0, t0:t0 + c, :] = cum
        for term, (dt_t, cum_t) in enumerate(zip(_bf16_terms(dt), _bf16_terms(cum))):
            dt3_ref[0, term, t0:t0 + c, :] = dt_t
            cum3_ref[0, term, t0:t0 + c, :] = cum_t


def _ssd_prep(small, bias, a):
    b, s, _ = small.shape
    out = jax.ShapeDtypeStruct((b, s, LANES), F32)
    out3 = jax.ShapeDtypeStruct((b, 3, s, LANES), BF16)
    spec = pl.BlockSpec((1, s, LANES), lambda i: (i, 0, 0))
    spec3 = pl.BlockSpec((1, 3, s, LANES), lambda i: (i, 0, 0, 0))
    return pl.pallas_call(
        _ssd_prep_kernel,
        grid=(b,),
        in_specs=[spec,
                  pl.BlockSpec((1, LANES), lambda i: (0, 0)),
                  pl.BlockSpec((1, LANES), lambda i: (0, 0))],
        out_specs=[spec, spec, spec3, spec3],
        out_shape=[out, out, out3, out3],
        compiler_params=_cparams("parallel"),
        name="ssd_prep",
    )(small, bias, a)


def _ssd_kernel(xs_ref, bm_ref, cm_ref, z_ref, cum3_ref, dt3_ref, cumr_ref, dtr_ref, e_seg_ref, e_ch_ref,
                dsk_ref, nw_ref, o_ref, s_ref, y_ref, *, l, lc):
    c = SSD_CHUNK
    p = SSD_HEADDIM
    nh = SSD_HPG
    n_lat, n_ctx = l // c, lc // c
    n_all = n_lat + n_ctx
    row = lax.broadcasted_iota(jnp.int32, (c, c), 0)
    colm = lax.broadcasted_iota(jnp.int32, (c, c), 1)
    masks = (row >= colm, row <= colm)

    def chunk(ci, d, emit):
        tok = pl.ds(pl.multiple_of(ci * c, c), c)
        last = c - 1 if d == 0 else 0
        x = xs_ref[0, tok, :]
        bmat = bm_ref[0, tok, :]
        cum3 = cum3_ref[0, d, tok, :]
        cum_ch = jnp.dot(cum3, e_ch_ref[...], preferred_element_type=F32)
        dt_ch = jnp.dot(dt3_ref[0, d, tok, :], e_ch_ref[...], preferred_element_type=F32)
        cum_end = cum_ch[last:last + 1, :]
        state = s_ref[d]
        if emit:
            cmat = cm_ref[0, tok, :]
            cumr = cumr_ref[0, d, ci]
            dtr = dtr_ref[0, d, ci]
            cum_seg = jnp.dot(cum3, e_seg_ref[...], preferred_element_type=F32)
            cb = lax.dot_general(cmat, bmat, NT_DIMS, preferred_element_type=F32)
            ys = []
            for h in range(nh):
                seg = cum_seg[:, h * c:(h + 1) * c] - cumr[h:h + 1, :]
                decay = jnp.exp(jnp.where(masks[d], seg, -jnp.inf))
                m = (cb * decay * dtr[h:h + 1, :]).astype(BF16)
                ys.append(jnp.dot(m, x[:, h * p:(h + 1) * p], preferred_element_type=F32))
            y = jnp.concatenate(ys, axis=1)
            y = y + jnp.dot(cmat, state.astype(BF16), preferred_element_type=F32) * jnp.exp(cum_ch)
            y_ref[tok, :] = y_ref[tok, :] + y
        xw = (x.astype(F32) * (jnp.exp(cum_end - cum_ch) * dt_ch)).astype(BF16)
        s_ref[d] = state * jnp.exp(cum_end) + lax.dot_general(bmat, xw, TN_DIMS,
                                                              preferred_element_type=F32)

    def ctx_step(i, carry):
        chunk(n_lat + i, 0, False)
        chunk(n_all - 1 - i, 1, False)
        return carry

    def lat_step(i, carry):
        chunk(i, 0, True)
        chunk(n_lat - 1 - i, 1, True)
        return carry

    rows = 256
    for t0 in range(0, l, rows):
        y_ref[t0:t0 + rows, :] = dsk_ref[0] * xs_ref[0, t0:t0 + rows, :].astype(F32)
    s_ref[...] = jnp.zeros_like(s_ref)
    lax.fori_loop(0, n_ctx, ctx_step, 0)
    lax.fori_loop(0, n_lat, lat_step, 0)

    rows = 256
    for t0 in range(0, l, rows):
        y = y_ref[t0:t0 + rows, :] * _silu(z_ref[0, t0:t0 + rows, :].astype(F32))
        ms = jnp.mean(y * y, axis=-1, keepdims=True)
        o_ref[0, t0:t0 + rows, :] = (y * lax.rsqrt(ms + EPS) * nw_ref[0]).astype(o_ref.dtype)


SPLIT_K = 32


def _split3_cols(t3, b, s):
    g, hpg = SSD_GROUPS, SSD_HPG
    parts = jnp.concatenate([t3, jnp.zeros_like(t3[:, :1])], axis=1)
    parts = parts.reshape(b, SPLIT_K // hpg, s, 2, g, hpg).transpose(0, 4, 3, 2, 1, 5)
    return parts.reshape(b * g, 2, s, SPLIT_K)


def _expansion(width):
    k = jnp.arange(SPLIT_K)[:, None]
    n = jnp.arange(SSD_HPG * width)[None, :]
    return ((k % SSD_HPG == n // width) & (k < 3 * SSD_HPG)).astype(BF16)


def _ssd(xbc, proj, z_block0, cum3, dt3, cumr, dtr, dskip, norm_w, l, lc):
    b, s, _ = xbc.shape
    g = SSD_GROUPS
    gw = SSD_HPG * SSD_HEADDIM
    nchunks = s // SSD_CHUNK
    xs_blocks = g * gw // SSD_STATE
    e_seg = _expansion(SSD_CHUNK)
    e_ch = _expansion(SSD_HEADDIM)
    kern = functools.partial(_ssd_kernel, l=l, lc=lc)
    col_spec = pl.BlockSpec((1, 2, s, SPLIT_K), lambda i, j: (i * g + j, 0, 0, 0))
    row_spec = pl.BlockSpec((1, 2, nchunks, SSD_HPG, SSD_CHUNK), lambda i, j: (i * g + j, 0, 0, 0, 0))
    return pl.pallas_call(
        kern,
        grid=(b, g),
        in_specs=[pl.BlockSpec((1, s, gw), lambda i, j: (i, 0, j)),
                  pl.BlockSpec((1, s, SSD_STATE), lambda i, j: (i, 0, xs_blocks + j)),
                  pl.BlockSpec((1, s, SSD_STATE), lambda i, j: (i, 0, xs_blocks + g + j)),
                  pl.BlockSpec((1, l, gw), lambda i, j: (i, 0, z_block0 + j)),
                  col_spec, col_spec, row_spec, row_spec,
                  pl.BlockSpec(e_seg.shape, lambda i, j: (0, 0)),
                  pl.BlockSpec(e_ch.shape, lambda i, j: (0, 0)),
                  pl.BlockSpec((1, 1, gw), lambda i, j: (j, 0, 0)),
                  pl.BlockSpec((1, 1, gw), lambda i, j: (j, 0, 0))],
        out_specs=pl.BlockSpec((1, l, gw), lambda i, j: (i, 0, j)),
        out_shape=jax.ShapeDtypeStruct((b, l, g * gw), BF16),
        scratch_shapes=[pltpu.VMEM((2, SSD_STATE, gw), F32),
                        pltpu.VMEM((l, gw), F32)],
        compiler_params=_cparams("parallel", "parallel"),
        name="ssd_scan",
    )(xbc, xbc, xbc, proj, cum3, dt3, cumr, dtr, e_seg, e_ch, dskip, norm_w)


GLA_TILE = 256


def _split_dot(m, x):
    x_hi = x.astype(BF16)
    x_lo = (x - x_hi.astype(F32)).astype(BF16)
    return (jnp.dot(m, x_hi, preferred_element_type=F32) + jnp.dot(m, x_lo, preferred_element_type=F32))


def _gla_kernel(q_ref, k_ref, v_ref, r_ref, lr_ref, w2f_ref, b2f_ref, w2b_ref, b2b_ref, nw_ref,
                o_ref, st_ref, y_ref, qd_ref, kt_ref, eb_ref, *, l, lc):
    c = GLA_CHUNK
    t = GLA_TILE
    n_lat, n_ctx = l // c, lc // c
    n_all = n_lat + n_ctx
    hk = q_ref.shape[2]
    scale = hk ** -0.5
    row = lax.broadcasted_iota(jnp.int32, (t, t), 0)
    colm = lax.broadcasted_iota(jnp.int32, (t, t), 1)
    same_chunk = (row // c) == (colm // c)
    masks = (same_chunk & (row >= colm), same_chunk & (row <= colm))
    chunk_ones = same_chunk.astype(BF16)

    def gates(off, emit):
        tok = pl.ds(off, t)
        kf = k_ref[0, tok, :].astype(F32)
        lr_all = lr_ref[0, tok, :]
        for d in range(2):
            w2 = w2f_ref if d == 0 else w2b_ref
            b2 = b2f_ref if d == 0 else b2b_ref
            lr = lr_all[:, d * GLA_RANK:(d + 1) * GLA_RANK]
            logit = jnp.dot(lr, w2[...], precision=HIGHEST, preferred_element_type=F32) + b2[...]
            g = -_softplus(-logit) * (1.0 / GLA_NORMALIZER)
            bcum = _split_dot(masks[d].astype(BF16), g)
            btot = _split_dot(chunk_ones, g)
            kt_ref[d, tok, :] = (kf * jnp.exp(btot - bcum)).astype(BF16)
            ebt = jnp.exp(btot)
            for j in range(t // c):
                eb_ref[d, pl.ds(off // c + j, 1)] = ebt[j * c:j * c + 8, :][None]
            if emit:
                qd = (q_ref[0, tok, :].astype(F32) * scale * jnp.exp(bcum)).astype(BF16)
                kd = (kf * jnp.exp(-bcum)).astype(BF16)
                a = lax.dot_general(qd, kd, NT_DIMS, preferred_element_type=F32)
                a = jnp.where(masks[d], a, 0.0).astype(BF16)
                o = jnp.dot(a, v_ref[0, tok, :], preferred_element_type=F32)
                qd_ref[d, tok, :] = qd
                if d == 0:
                    y_ref[tok, :] = o
                else:
                    y_ref[tok, :] = y_ref[tok, :] + o

    def lat_gates(i, carry):
        gates(pl.multiple_of(i * t, t), True)
        return carry

    lax.fori_loop(0, l // t, lat_gates, 0)
    for i in range(lc // t):
        gates(l + i * t, False)

    def step(ci, d, emit):
        tok = pl.ds(pl.multiple_of(ci * c, c), c)
        st = st_ref[d]
        if emit:
            o = lax.dot_general(qd_ref[d, tok, :], st.astype(BF16), NT_DIMS, preferred_element_type=F32)
            y_ref[tok, :] = y_ref[tok, :] + o
        st_ref[d] = st * eb_ref[d, ci][0:1, :] + lax.dot_general(
            v_ref[0, tok, :], kt_ref[d, tok, :], TN_DIMS, preferred_element_type=F32)

    def ctx_step(i, carry):
        step(n_lat + i, 0, False)
        step(n_all - 1 - i, 1, False)
        return carry

    def lat_step(i, carry):
        step(i, 0, True)
        step(n_lat - 1 - i, 1, True)
        return carry

    st_ref[...] = jnp.zeros_like(st_ref)
    lax.fori_loop(0, n_ctx, ctx_step, 0)
    lax.fori_loop(0, n_lat, lat_step, 0)

    rows = 256
    for t0 in range(0, l, rows):
        o = y_ref[t0:t0 + rows, :]
        ms = jnp.mean(o * o, axis=-1, keepdims=True)
        o = o * lax.rsqrt(ms + EPS) * nw_ref[...]
        o_ref[0, t0:t0 + rows, :] = (o * _silu(r_ref[0, t0:t0 + rows, :].astype(F32))).astype(o_ref.dtype)


def _gla(proj, small, w2f, b2f, w2b, b2b, norm_w, l, lc, blocks):
    b, s, _ = proj.shape
    h = GLA_HEADS
    hk = w2f.shape[1] // h
    hv = norm_w.shape[0]
    q0, k0, v0, r0 = blocks
    kern = functools.partial(_gla_kernel, l=l, lc=lc)
    return pl.pallas_call(
        kern,
        grid=(b, h),
        in_specs=[pl.BlockSpec((1, s, hk), lambda i, j: (i, 0, q0 // hk + j)),
                  pl.BlockSpec((1, s, hk), lambda i, j: (i, 0, k0 // hk + j)),
                  pl.BlockSpec((1, s, hv), lambda i, j: (i, 0, v0 // hv + j)),
                  pl.BlockSpec((1, l, hv), lambda i, j: (i, 0, r0 // hv + j)),
                  pl.BlockSpec((1, s, LANES), lambda i, j: (i, 0, 1)),
                  pl.BlockSpec((GLA_RANK, hk), lambda i, j: (0, j)),
                  pl.BlockSpec((1, hk), lambda i, j: (0, j)),
                  pl.BlockSpec((GLA_RANK, hk), lambda i, j: (0, j)),
                  pl.BlockSpec((1, hk), lambda i, j: (0, j)),
                  pl.BlockSpec((1, hv), lambda i, j: (0, 0))],
        out_specs=pl.BlockSpec((1, l, hv), lambda i, j: (i, 0, j)),
        out_shape=jax.ShapeDtypeStruct((b, l, h * hv), BF16),
        scratch_shapes=[pltpu.VMEM((2, hv, hk), F32),
                        pltpu.VMEM((l, hv), F32),
                        pltpu.VMEM((2, l, hk), BF16),
                        pltpu.VMEM((2, s, hk), BF16),
                        pltpu.VMEM((2, s // GLA_CHUNK, 8, hk), F32)],
        compiler_params=_cparams("parallel", "parallel"),
        name="gla_scan",
    )(proj, proj, proj, proj, small, w2f, b2f.reshape(1, -1), w2b, b2b.reshape(1, -1),
      norm_w.reshape(1, hv))


def _merge_kernel(a_ref, y_ref, wa_ref, wb_ref, gla_ref, glb_ref, ba_ref, bb_ref, o_ref):
    ya = jnp.dot(a_ref[0], wa_ref[...], preferred_element_type=F32)
    yb = jnp.dot(y_ref[0], wb_ref[...], preferred_element_type=F32)
    ga = jax.nn.sigmoid(gla_ref[0].astype(F32) + ba_ref[...])
    gb = jax.nn.sigmoid(glb_ref[0].astype(F32) + bb_ref[...])
    o_ref[0] = (ga * ya + gb * yb).astype(o_ref.dtype)


def _merge(a_n, y_n, w_a, w_b, proj, gl0, b_gate):
    b, l, ka = a_n.shape
    kb = y_n.shape[2]
    d = w_a.shape[1]
    bm = _pick(l, (512, 256))
    bn = 512
    nb = d // bn
    bg = b_gate.reshape(1, 2 * d)
    return pl.pallas_call(
        _merge_kernel,
        grid=(b, l // bm, nb),
        in_specs=[pl.BlockSpec((1, bm, ka), lambda i, t, j: (i, t, 0)),
                  pl.BlockSpec((1, bm, kb), lambda i, t, j: (i, t, 0)),
                  pl.BlockSpec((ka, bn), lambda i, t, j: (0, j)),
                  pl.BlockSpec((kb, bn), lambda i, t, j: (0, j)),
                  pl.BlockSpec((1, bm, bn), lambda i, t, j: (i, t, gl0 // bn + j)),
                  pl.BlockSpec((1, bm, bn), lambda i, t, j: (i, t, gl0 // bn + nb + j)),
                  pl.BlockSpec((1, bn), lambda i, t, j: (0, j)),
                  pl.BlockSpec((1, bn), lambda i, t, j: (0, nb + j))],
        out_specs=pl.BlockSpec((1, bm, bn), lambda i, t, j: (i, t, j)),
        out_shape=jax.ShapeDtypeStruct((b, l, d), BF16),
        compiler_params=_cparams("parallel", "parallel", "parallel"),
        name="branch_merge",
    )(a_n, y_n, w_a, w_b, proj, proj, bg, bg)


def _resid_kernel(m_ref, w_ref, x_ref, g_ref, o_ref):
    y = jnp.dot(m_ref[0], w_ref[...], preferred_element_type=F32)
    o_ref[0] = x_ref[0] + g_ref[0] * y


def _resid(m, w_o, x, gate):
    b, l, d = x.shape
    bm = _pick(l, (1024, 512, 256))
    bn = 512
    return pl.pallas_call(
        _resid_kernel,
        grid=(b, l // bm, d // bn),
        in_specs=[pl.BlockSpec((1, bm, d), lambda i, t, j: (i, t, 0)),
                  pl.BlockSpec((d, bn), lambda i, t, j: (0, j)),
                  pl.BlockSpec((1, bm, bn), lambda i, t, j: (i, t, j)),
                  pl.BlockSpec((1, 1, bn), lambda i, t, j: (i, 0, j))],
        out_specs=pl.BlockSpec((1, bm, bn), lambda i, t, j: (i, t, j)),
        out_shape=jax.ShapeDtypeStruct((b, l, d), F32),
        compiler_params=_cparams("parallel", "parallel", "parallel"),
        name="attn_resid",
    )(m, w_o, x, gate)


def _peer_q_kernel(x_ref, nw_ref, sh_ref, sc_ref, wq_ref, h_ref, q_ref, lhs_ref):
    @pl.when(pl.program_id(2) == 0)
    def _():
        xf = x_ref[0]
        ms = jnp.mean(xf * xf, axis=-1, keepdims=True)
        y = xf * lax.rsqrt(ms + EPS) * nw_ref[...]
        hb = (y * (1.0 + sc_ref[0]) + sh_ref[0]).astype(BF16)
        lhs_ref[...] = hb
        h_ref[0] = hb

    q_ref[0] = jnp.dot(lhs_ref[...], wq_ref[...], preferred_element_type=F32)


def _peer_q(x1, norm_w, shift, scale, wq):
    b, l, d = x1.shape
    n = wq.shape[1]
    bm = _pick(l, (512, 256))
    bn = n
    return pl.pallas_call(
        _peer_q_kernel,
        grid=(b, l // bm, n // bn),
        in_specs=[pl.BlockSpec((1, bm, d), lambda i, t, j: (i, t, 0)),
                  pl.BlockSpec((1, d), lambda i, t, j: (0, 0)),
                  pl.BlockSpec((1, 1, d), lambda i, t, j: (i, 0, 0)),
                  pl.BlockSpec((1, 1, d), lambda i, t, j: (i, 0, 0)),
                  pl.BlockSpec((d, bn), lambda i, t, j: (0, j))],
        out_specs=[pl.BlockSpec((1, bm, d), lambda i, t, j: (i, t, 0)),
                   pl.BlockSpec((1, bm, bn), lambda i, t, j: (i, t, j))],
        out_shape=[jax.ShapeDtypeStruct((b, l, d), BF16),
                   jax.ShapeDtypeStruct((b, l, n), F32)],
        scratch_shapes=[pltpu.VMEM((bm, d), BF16)],
        compiler_params=_cparams("parallel", "parallel", "arbitrary"),
        name="peer_query",
    )(x1, norm_w.reshape(1, d), shift, scale, wq)


def _top_values(x, k, rows):
    sub = lax.broadcasted_iota(jnp.int32, (rows, x.shape[1]), 0)
    out = jnp.full((rows, x.shape[1]), -jnp.inf, F32)
    for i in range(k):
        m = jnp.max(x, axis=0, keepdims=True)
        out = jnp.where(sub == i, m, out)
        x = jnp.where(x == m, -jnp.inf, x)
    return out


def _peer_score_kernel(q_ref, keys_ref, c1_ref, e1_ref, s2_ref, e2_ref):
    k = PEER_TOPK
    dk = keys_ref.shape[3]
    q = q_ref[...]
    s1 = lax.dot_general(keys_ref[0, 0], q[:, 0:dk], NT_DIMS, precision=HIGHEST,
                         preferred_element_type=F32)
    s2 = lax.dot_general(keys_ref[0, 1], q[:, dk:2 * dk], NT_DIMS, precision=HIGHEST,
                         preferred_element_type=F32)
    n = k + 1
    pad = -(-n // 8) * 8
    sv1 = _top_values(s1, n, pad)
    sv2 = _top_values(s2, n, pad)
    assert n // 2 <= 8 and n // 9 == 1
    cand = jnp.concatenate(
        [sv1[0:1, :] + sv2]
        + [sv1[a:a + 1, :] + sv2[0:8, :] for a in range(1, 8)]
        + [sv1[8:pad, :] + sv2[0:1, :]], axis=0)
    cv = _top_values(cand, n, pad)
    z = jnp.sum(jnp.exp(cv[0:k, :] - cv[0:1, :]), axis=0, keepdims=True)
    thr = 0.5 * (cv[k - 1:k, :] + cv[k:k + 1, :])
    c1_ref[0] = thr - s1
    s2_ref[0] = s2
    e1_ref[0] = jnp.exp(s1 - sv1[0:1, :]) / z
    e2_ref[0] = jnp.exp(s2 - sv2[0:1, :])


def _peer_scores(q, keys):
    t, n = q.shape
    h, _, nk, dk = keys.shape
    tt = _pick(t, (512, 256))
    big = jax.ShapeDtypeStruct((h, nk, t), F32)
    big_spec = pl.BlockSpec((1, nk, tt), lambda i, j: (j, 0, i))
    return pl.pallas_call(
        _peer_score_kernel,
        grid=(t // tt, h),
        in_specs=[pl.BlockSpec((tt, 2 * dk), lambda i, j: (i, j)),
                  pl.BlockSpec((1, 2, nk, dk), lambda i, j: (j, 0, 0, 0))],
        out_specs=[big_spec, big_spec, big_spec, big_spec],
        out_shape=[big, big, big, big],
        compiler_params=_cparams("parallel", "parallel"),
        name="peer_scores",
    )(q, keys)


PEER_EXPERT_BLOCK = 1024
GATE_ROWS = 8


def _peer_expert_kernel(h_ref, u_ref, vt_ref, c1_ref, e1_ref, s2_ref, e2_ref,
                        o_ref, ht_ref, w_ref, c8_ref, e8_ref, *, n_sub):
    j = pl.program_id(1)
    nh, nk, bm = s2_ref.shape
    sub = 8

    @pl.when(j == 0)
    def _():
        o_ref[...] = jnp.zeros_like(o_ref)
        ht_ref[...] = h_ref[...].astype(F32).T.astype(BF16)

    i1_0 = pl.multiple_of(j * n_sub, n_sub)
    for hd in range(nh):
        cgrp = c1_ref[hd, pl.ds(i1_0, n_sub), :]
        egrp = e1_ref[hd, pl.ds(i1_0, n_sub), :]
        for sb in range(n_sub):
            r0 = (hd * n_sub + sb) * sub
            c8_ref[r0:r0 + sub, :] = jnp.broadcast_to(cgrp[sb:sb + 1, :], (sub, bm))
            e8_ref[r0:r0 + sub, :] = jnp.broadcast_to(egrp[sb:sb + 1, :], (sub, bm))

    act = jnp.dot(u_ref[...], ht_ref[...], preferred_element_type=F32)

    n_part = GATE_ROWS // sub

    for lc in range(bm // LANES):
        ls = slice(lc * LANES, (lc + 1) * LANES)
        for rc in range(nk // GATE_ROWS):
            base = rc * GATE_ROWS
            accs = [[jnp.zeros((sub, LANES), F32) for _ in range(n_part)] for _ in range(n_sub)]
            for hd in range(nh):
                s2v = [s2_ref[hd, base + k * sub:base + (k + 1) * sub, ls] for k in range(n_part)]
                e2v = [e2_ref[hd, base + k * sub:base + (k + 1) * sub, ls] for k in range(n_part)]
                for sb in range(n_sub):
                    r0 = (hd * n_sub + sb) * sub
                    cut = c8_ref[r0:r0 + sub, ls]
                    e1v = e8_ref[r0:r0 + sub, ls]
                    for k in range(n_part):
                        accs[sb][k] = accs[sb][k] + jnp.where(s2v[k] >= cut, e2v[k], 0.0) * e1v
            for sb in range(n_sub):
                for k in range(n_part):
                    r0 = sb * nk + base + k * sub
                    w_ref[r0:r0 + sub, ls] = accs[sb][k]

    gelu = 0.5 * act * (1.0 + lax.erf(act * (2.0 ** -0.5)))
    pmat = (gelu * w_ref[...]).astype(BF16)
    o_ref[...] += jnp.dot(vt_ref[...], pmat, preferred_element_type=F32)


def _peer_experts(h2, u, vt, c1, e1, s2, e2):
    t, d = h2.shape
    e = u.shape[0]
    nh, nk, _ = s2.shape
    bm = _pick(t, (512, 256))
    eb = PEER_EXPERT_BLOCK
    n_sub = eb // nk
    kern = functools.partial(_peer_expert_kernel, n_sub=n_sub)
    big_spec = pl.BlockSpec((nh, nk, bm), lambda i, j: (0, 0, i))
    return pl.pallas_call(
        kern,
        grid=(t // bm, e // eb),
        in_specs=[pl.BlockSpec((bm, d), lambda i, j: (i, 0)),
                  pl.BlockSpec((eb, d), lambda i, j: (j, 0)),
                  pl.BlockSpec((d, eb), lambda i, j: (0, j)),
                  big_spec, big_spec, big_spec, big_spec],
        out_specs=pl.BlockSpec((d, bm), lambda i, j: (0, i)),
        out_shape=jax.ShapeDtypeStruct((d, t), F32),
        scratch_shapes=[pltpu.VMEM((d, bm), BF16),
                        pltpu.VMEM((eb, bm), F32),
                        pltpu.VMEM((nh * n_sub * 8, bm), F32),
                        pltpu.VMEM((nh * n_sub * 8, bm), F32)],
        compiler_params=_cparams("parallel", "arbitrary"),
        name="peer_experts",
    )(h2, u, vt, c1, e1, s2, e2)


def _final_kernel(x_ref, yt_ref, g_ref, w_ref, o_ref):
    xf = x_ref[0] + g_ref[0] * yt_ref[...].T
    ms = jnp.mean(xf * xf, axis=-1, keepdims=True)
    o_ref[0] = xf * lax.rsqrt(ms + EPS) * w_ref[...]


def _final(x1, yt, gate, w):
    b, l, d = x1.shape
    bm = _pick(l, (512, 256))
    nt = l // bm
    return pl.pallas_call(
        _final_kernel,
        grid=(b, l // bm),
        in_specs=[pl.BlockSpec((1, bm, d), lambda i, t: (i, t, 0)),
                  pl.BlockSpec((d, bm), lambda i, t: (0, i * nt + t)),
                  pl.BlockSpec((1, 1, d), lambda i, t: (i, 0, 0)),
                  pl.BlockSpec((1, d), lambda i, t: (0, 0))],
        out_specs=pl.BlockSpec((1, bm, d), lambda i, t: (i, t, 0)),
        out_shape=jax.ShapeDtypeStruct((b, l, d), F32),
        compiler_params=_cparams("parallel", "parallel"),
        name="final_norm",
    )(x1, yt, gate, w.reshape(1, d))


def _layer(x, ctx, mod_x, mod_c, p):
    b, l, d = x.shape
    lc = ctx.shape[1]
    s = l + lc
    dk = p['w_lr2_f'].shape[1]
    dv = p['w_gla_out'].shape[0]
    di = p['w_ssd_out'].shape[0]
    bc = SSD_GROUPS * SSD_STATE
    n_ssd_heads = p['a_log_f'].shape[0]

    sizes = (dk, dk, dv, dv, GLA_RANK, GLA_RANK, di, di + 2 * bc, n_ssd_heads, n_ssd_heads, 2 * d)
    offs = [0]
    for sz in sizes:
        offs.append(offs[-1] + sz)
    w_in = p['w_in']
    seg = lambda i: w_in[:, offs[i]:offs[i + 1]]
    w_main = jnp.concatenate([seg(0), seg(1), seg(2), seg(3), seg(6), seg(7), seg(10)], axis=1).astype(BF16)
    n_small = 2 * LANES
    w_small = jnp.concatenate(
        [seg(8), seg(9), seg(4), seg(5),
         jnp.zeros((d, n_small - 2 * n_ssd_heads - 2 * GLA_RANK), w_in.dtype)], axis=1).astype(BF16)
    q0, k0, v0, r0 = 0, dk, 2 * dk, 2 * dk + dv
    z0 = r0 + dv
    xbc0 = z0 + di
    gl0 = xbc0 + di + 2 * bc

    shift1 = jnp.stack([mod_x[0], jnp.broadcast_to(mod_c[0], (b, d))], axis=1).reshape(b, 2, 1, d)
    scale1 = jnp.stack([mod_x[1], jnp.broadcast_to(mod_c[1], (b, d))], axis=1).reshape(b, 2, 1, d)
    h = _norm_mod(x, ctx, p['norm1_w'], shift1, scale1)

    h2d = h.reshape(b * s, d)
    proj = _matmul(h2d, w_main, BF16, "in_proj").reshape(b, s, -1)
    small = _matmul(h2d, w_small, F32, "in_proj_gates").reshape(b, s, n_small)

    xbc = _conv(proj, xbc0 // 512, di + 2 * bc, p['conv_w'], p['conv_b'], l, lc)
    bias = jnp.concatenate([p['dt_bias_f'], p['dt_bias_b']]).reshape(1, LANES).astype(F32)
    a_neg = -jnp.exp(jnp.concatenate([p['a_log_f'], p['a_log_b']]).astype(F32)).reshape(1, LANES)
    dt, cum, dt3, cum3 = _ssd_prep(small, bias, a_neg)
    g, hpg, c = SSD_GROUPS, SSD_HPG, SSD_CHUNK
    to_rows = lambda t: t.reshape(b, s // c, c, 2, g, hpg).transpose(0, 4, 3, 1, 5, 2).reshape(
        b * g, 2, s // c, hpg, c)
    dskip = jnp.repeat(p['d_skip'].astype(F32), SSD_HEADDIM).reshape(g, 1, hpg * SSD_HEADDIM)
    ssd_nw = p['ssd_norm_w'].astype(F32).reshape(g, 1, hpg * SSD_HEADDIM)
    y_n = _ssd(xbc, proj, z0 // 512, _split3_cols(cum3, b, s), _split3_cols(dt3, b, s),
               to_rows(cum), to_rows(dt), dskip, ssd_nw, l, lc)

    a_n = _gla(proj, small, p['w_lr2_f'], p['b_lr_f'], p['w_lr2_b'], p['b_lr_b'], p['gla_norm_w'],
               l, lc, (q0, k0, v0, r0))

    m = _merge(a_n, y_n, p['w_gla_out'].astype(BF16), p['w_ssd_out'].astype(BF16), proj, gl0,
               p['b_gate'])
    return _resid(m, p['w_o'].astype(BF16), x, mod_x[2].reshape(b, 1, d))


def _peer(x1, mod_x, norm_w, wq, keys, u, v):
    b, l, d = x1.shape
    h2, q = _peer_q(x1, norm_w, mod_x[3].reshape(b, 1, d), mod_x[4].reshape(b, 1, d), wq.astype(BF16))
    c1, e1, s2, e2 = _peer_scores(q.reshape(b * l, -1), keys)
    return _peer_experts(h2.reshape(b * l, d), u.astype(BF16), v.astype(BF16).T, c1, e1, s2, e2)


def kernel(x, c, ctx, c_ctx, w_ada, b_ada, norm1_w, w_in, b_gate, w_lr2_f, b_lr_f, w_lr2_b, b_lr_b,
           gla_norm_w, w_gla_out, conv_w, conv_b, a_log_f, a_log_b, dt_bias_f, dt_bias_b, d_skip,
           ssd_norm_w, w_ssd_out, w_o, norm2_w, peer_wq, peer_keys, peer_u, peer_v, final_norm_w):
    b, l, d = x.shape
    depth = w_in.shape[0]
    assert depth == 1, "context-stream update for deeper stacks is not implemented"
    layer = 0
    rows = -(-(b + 1) // 8) * 8
    c_all = jnp.concatenate([c, c_ctx[None, :], jnp.zeros((rows - b - 1, d), c.dtype)], axis=0)
    mod = _ada(c_all, w_ada[layer], b_ada[layer])
    mod_x = [mod[:b, i * d:(i + 1) * d] for i in range(N_MOD)]
    mod_c = [mod[b, i * d:(i + 1) * d] for i in range(N_MOD)]
    p = {
        'norm1_w': norm1_w[layer], 'w_in': w_in[layer], 'b_gate': b_gate[layer],
        'w_lr2_f': w_lr2_f[layer], 'b_lr_f': b_lr_f[layer],
        'w_lr2_b': w_lr2_b[layer], 'b_lr_b': b_lr_b[layer],
        'gla_norm_w': gla_norm_w[layer], 'w_gla_out': w_gla_out[layer],
        'conv_w': conv_w[layer], 'conv_b': conv_b[layer],
        'a_log_f': a_log_f[layer], 'a_log_b': a_log_b[layer],
        'dt_bias_f': dt_bias_f[layer], 'dt_bias_b': dt_bias_b[layer],
        'd_skip': d_skip[layer], 'ssd_norm_w': ssd_norm_w[layer],
        'w_ssd_out': w_ssd_out[layer], 'w_o': w_o[layer],
    }
    x1 = _layer(x, ctx, mod_x, mod_c, p)
    y = _peer(x1, mod_x, norm2_w[layer], peer_wq[layer], peer_keys[layer], peer_u[layer], peer_v[layer])
    return _final(x1, y, mod_x[5].reshape(b, 1, d), final_norm_w)
```

```python
import functools

import jax
import jax.numpy as jnp
from jax import lax
from jax.experimental import pallas as pl
from jax.experimental.pallas import tpu as pltpu

F32 = jnp.float32
BF16 = jnp.bfloat16
EPS = 1e-6
HIGHEST = lax.Precision.HIGHEST

N_MOD = 6
GRID_W = 64
GLA_HEADS = 4
GLA_RANK = 16
GLA_NORMALIZER = 16.0
GLA_CHUNK = 64
SSD_HEADDIM = 64
SSD_GROUPS = 8
SSD_HPG = 8
SSD_STATE = 128
SSD_CHUNK = 128
PEER_HEADS = 8
PEER_NKEYS = 128
PEER_TOPK = 16

VMEM_LIMIT_BYTES = 56 * 1024 * 1024
LANES = 128

NT_DIMS = (((1,), (1,)), ((), ()))
TN_DIMS = (((0,), (0,)), ((), ()))


def _cparams(*sem, flags=None):
    return pltpu.CompilerParams(dimension_semantics=sem, vmem_limit_bytes=VMEM_LIMIT_BYTES, flags=flags)


def _pick(n, options):
    for o in options:
        if n % o == 0:
            return o
    raise ValueError(f"no tile in {options} divides {n}")


def _softplus(x):
    return jnp.maximum(x, 0.0) + jnp.log1p(jnp.exp(-jnp.abs(x)))


def _silu(x):
    return x * jax.nn.sigmoid(x)


def _ada_kernel(c_ref, w_ref, b_ref, o_ref):
    a = _silu(c_ref[...])
    o_ref[...] = jnp.dot(a, w_ref[...], precision=HIGHEST, preferred_element_type=F32) + b_ref[...]


def _ada(c_all, w_ada, b_ada):
    m, d = c_all.shape
    n = w_ada.shape[1]
    bn = _pick(n, (1024, 512, 256, 128))
    return pl.pallas_call(
        _ada_kernel,
        grid=(n // bn,),
        in_specs=[pl.BlockSpec((m, d), lambda j: (0, 0)),
                  pl.BlockSpec((d, bn), lambda j: (0, j)),
                  pl.BlockSpec((1, bn), lambda j: (0, j))],
        out_specs=pl.BlockSpec((m, bn), lambda j: (0, j)),
        out_shape=jax.ShapeDtypeStruct((m, n), F32),
        compiler_params=_cparams("parallel"),
        name="ada_mod",
    )(c_all, w_ada, b_ada.reshape(1, n))


def _norm_mod_kernel(x_ref, ctx_ref, w_ref, sh_ref, sc_ref, o_ref, *, n_lat_tiles):
    j = pl.program_id(1)

    def emit(src):
        xf = src[0]
        ms = jnp.mean(xf * xf, axis=-1, keepdims=True)
        y = xf * lax.rsqrt(ms + EPS) * w_ref[...]
        o_ref[0] = (y * (1.0 + sc_ref[0, 0]) + sh_ref[0, 0]).astype(o_ref.dtype)

    @pl.when(j < n_lat_tiles)
    def _():
        emit(x_ref)

    @pl.when(j >= n_lat_tiles)
    def _():
        emit(ctx_ref)


def _norm_mod(x, ctx, w, shift, scale):
    b, l, d = x.shape
    lc = ctx.shape[1]
    tn = _pick(lc, (256, 128))
    nl, nc = l // tn, lc // tn
    kern = functools.partial(_norm_mod_kernel, n_lat_tiles=nl)
    return pl.pallas_call(
        kern,
        grid=(b, nl + nc),
        in_specs=[pl.BlockSpec((1, tn, d), lambda i, j: (i, jnp.minimum(j, nl - 1), 0)),
                  pl.BlockSpec((1, tn, d), lambda i, j: (i, jnp.maximum(j - nl, 0), 0)),
                  pl.BlockSpec((1, d), lambda i, j: (0, 0)),
                  pl.BlockSpec((1, 1, 1, d), lambda i, j: (i, j // nl, 0, 0)),
                  pl.BlockSpec((1, 1, 1, d), lambda i, j: (i, j // nl, 0, 0))],
        out_specs=pl.BlockSpec((1, tn, d), lambda i, j: (i, j, 0)),
        out_shape=jax.ShapeDtypeStruct((b, l + lc, d), BF16),
        compiler_params=_cparams("parallel", "parallel"),
        name="norm_mod",
    )(x, ctx, w.reshape(1, d), shift, scale)


def _matmul_kernel(a_ref, b_ref, o_ref):
    o_ref[...] = jnp.dot(a_ref[...], b_ref[...], preferred_element_type=F32).astype(o_ref.dtype)


def _matmul(a, b, out_dtype, name):
    m, k = a.shape
    n = b.shape[1]
    bm = _pick(m, (1024, 512, 256))
    bn = _pick(n, (1024, 512, 256))
    return pl.pallas_call(
        _matmul_kernel,
        grid=(m // bm, n // bn),
        in_specs=[pl.BlockSpec((bm, k), lambda i, j: (i, 0)),
                  pl.BlockSpec((k, bn), lambda i, j: (0, j))],
        out_specs=pl.BlockSpec((bm, bn), lambda i, j: (i, j)),
        out_shape=jax.ShapeDtypeStruct((m, n), out_dtype),
        compiler_params=_cparams("parallel", "parallel"),
        name=name,
    )(a, b)


CONV_TOK = 256
CTX_HALO = 8
LAT_HALO = GRID_W + 8


def _conv_kernel(x_ref, w_ref, b_ref, o_ref, lat_ref, ctx_ref, *, l, lc):
    tc = x_ref.shape[2]
    lat_ref[0:LAT_HALO, :] = jnp.zeros((LAT_HALO, tc), F32)
    lat_ref[LAT_HALO + l:LAT_HALO + l + LAT_HALO, :] = jnp.zeros((LAT_HALO, tc), F32)
    ctx_ref[0:CTX_HALO, :] = jnp.zeros((CTX_HALO, tc), F32)
    ctx_ref[CTX_HALO + lc:CTX_HALO + lc + CTX_HALO, :] = jnp.zeros((CTX_HALO, tc), F32)
    lat_ref[LAT_HALO:LAT_HALO + l, :] = x_ref[0, 0:l, :].astype(F32)
    ctx_ref[CTX_HALO:CTX_HALO + lc, :] = x_ref[0, l:l + lc, :].astype(F32)

    bias = b_ref[...]
    col = lax.broadcasted_iota(jnp.int32, (CONV_TOK, tc), 0) % GRID_W
    not_first = col != 0
    not_last = col != GRID_W - 1

    for t0 in range(0, l, CONV_TOK):
        acc = jnp.zeros((CONV_TOK, tc), F32)
        for kw in range(3):
            part = jnp.zeros((CONV_TOK, tc), F32)
            for kh in range(3):
                start = LAT_HALO + t0 + GRID_W * (kh - 1) + (kw - 1)
                part = part + w_ref[kh * 3 + kw:kh * 3 + kw + 1, :] * lat_ref[start:start + CONV_TOK, :]
            if kw == 0:
                part = jnp.where(not_first, part, 0.0)
            elif kw == 2:
                part = jnp.where(not_last, part, 0.0)
            acc = acc + part
        o_ref[0, t0:t0 + CONV_TOK, :] = _silu(acc + bias).astype(o_ref.dtype)

    ctx_tok = min(CONV_TOK, lc)
    for t0 in range(0, lc, ctx_tok):
        acc = jnp.zeros((ctx_tok, tc), F32)
        for kw in range(3):
            start = CTX_HALO + t0 + (kw - 1)
            acc = acc + w_ref[3 + kw:4 + kw, :] * ctx_ref[start:start + ctx_tok, :]
        o_ref[0, l + t0:l + t0 + ctx_tok, :] = _silu(acc + bias).astype(o_ref.dtype)


def _conv(proj, col_block0, n_ch, conv_w, conv_b, l, lc):
    b, s, _ = proj.shape
    tc = 512
    kern = functools.partial(_conv_kernel, l=l, lc=lc)
    return pl.pallas_call(
        kern,
        grid=(b, n_ch // tc),
        in_specs=[pl.BlockSpec((1, s, tc), lambda i, j: (i, 0, col_block0 + j)),
                  pl.BlockSpec((9, tc), lambda i, j: (0, j)),
                  pl.BlockSpec((1, tc), lambda i, j: (0, j))],
        out_specs=pl.BlockSpec((1, s, tc), lambda i, j: (i, 0, j)),
        out_shape=jax.ShapeDtypeStruct((b, s, n_ch), BF16),
        scratch_shapes=[pltpu.VMEM((l + 2 * LAT_HALO, tc), F32),
                        pltpu.VMEM((lc + 2 * CTX_HALO, tc), F32)],
        compiler_params=_cparams("parallel", "parallel"),
        name="dwconv_silu",
    )(proj, conv_w.reshape(9, n_ch), conv_b.reshape(1, n_ch))


def _bf16_terms(x):
    hi = x.astype(BF16)
    r1 = x - hi.astype(F32)
    mid = r1.astype(BF16)
    lo = (r1 - mid.astype(F32)).astype(BF16)
    return hi, mid, lo


def _ssd_prep_kernel(x_ref, bias_ref, a_ref, dt_ref, cum_ref, dt3_ref, cum3_ref):
    c = SSD_CHUNK
    s = x_ref.shape[1]
    row = lax.broadcasted_iota(jnp.int32, (c, c), 0)
    colm = lax.broadcasted_iota(jnp.int32, (c, c), 1)
    tri_f = (row >= colm).astype(F32)
    tri_b = (row <= colm).astype(F32)
    is_fwd = lax.broadcasted_iota(jnp.int32, (c, LANES), 1) < (LANES // 2)
    for t0 in range(0, s, c):
        dt = _softplus(x_ref[0, t0:t0 + c, :] + bias_ref[...])
        dta = dt * a_ref[...]
        cf = jnp.dot(tri_f, dta, precision=HIGHEST, preferred_element_type=F32)
        cb = jnp.dot(tri_b, dta, precision=HIGHEST, preferred_element_type=F32)
        cum = jnp.where(is_fwd, cf, cb)
        dt_ref[0, t0:t0 + c, :] = dt
        cum_ref[0, t0:t0 + c, :] = cum
        for term, (dt_t, cum_t) in enumerate(zip(_bf16_terms(dt), _bf16_terms(cum))):
            dt3_ref[0, term, t0:t0 + c, :] = dt_t
            cum3_ref[0, term, t0:t0 + c, :] = cum_t


def _ssd_prep(small, bias, a):
    b, s, _ = small.shape
    out = jax.ShapeDtypeStruct((b, s, LANES), F32)
    out3 = jax.ShapeDtypeStruct((b, 3, s, LANES), BF16)
    spec = pl.BlockSpec((1, s, LANES), lambda i: (i, 0, 0))
    spec3 = pl.BlockSpec((1, 3, s, LANES), lambda i: (i, 0, 0, 0))
    return pl.pallas_call(
        _ssd_prep_kernel,
        grid=(b,),
        in_specs=[spec,
                  pl.BlockSpec((1, LANES), lambda i: (0, 0)),
                  pl.BlockSpec((1, LANES), lambda i: (0, 0))],
        out_specs=[spec, spec, spec3, spec3],
        out_shape=[out, out, out3, out3],
        compiler_params=_cparams("parallel"),
        name="ssd_prep",
    )(small, bias, a)


def _ssd_kernel(xs_ref, bm_ref, cm_ref, z_ref, cum3_ref, dt3_ref, cumr_ref, dtr_ref, e_seg_ref, e_ch_ref,
                dsk_ref, nw_ref, o_ref, s_ref, y_ref, *, l, lc):
    c = SSD_CHUNK
    p = SSD_HEADDIM
    nh = SSD_HPG
    n_lat, n_ctx = l // c, lc // c
    n_all = n_lat + n_ctx
    row = lax.broadcasted_iota(jnp.int32, (c, c), 0)
    colm = lax.broadcasted_iota(jnp.int32, (c, c), 1)
    masks = (row >= colm, row <= colm)

    def chunk(ci, d, emit):
        tok = pl.ds(pl.multiple_of(ci * c, c), c)
        last = c - 1 if d == 0 else 0
        x = xs_ref[0, tok, :]
        bmat = bm_ref[0, tok, :]
        cum3 = cum3_ref[0, d, tok, :]
        cum_ch = jnp.dot(cum3, e_ch_ref[...], preferred_element_type=F32)
        dt_ch = jnp.dot(dt3_ref[0, d, tok, :], e_ch_ref[...], preferred_element_type=F32)
        cum_end = cum_ch[last:last + 1, :]
        state = s_ref[d]
        if emit:
            cmat = cm_ref[0, tok, :]
            cumr = cumr_ref[0, d, ci]
            dtr = dtr_ref[0, d, ci]
            cum_seg = jnp.dot(cum3, e_seg_ref[...], preferred_element_type=F32)
            cb = lax.dot_general(cmat, bmat, NT_DIMS, preferred_element_type=F32)
            ys = []
            for h in range(nh):
                seg = cum_seg[:, h * c:(h + 1) * c] - cumr[h:h + 1, :]
                decay = jnp.exp(jnp.where(masks[d], seg, -jnp.inf))
                m = (cb * decay * dtr[h:h + 1, :]).astype(BF16)
                ys.append(jnp.dot(m, x[:, h * p:(h + 1) * p], preferred_element_type=F32))
            y = jnp.concatenate(ys, axis=1)
            y = y + jnp.dot(cmat, state.astype(BF16), preferred_element_type=F32) * jnp.exp(cum_ch)
            y_ref[tok, :] = y_ref[tok, :] + y
        xw = (x.astype(F32) * (jnp.exp(cum_end - cum_ch) * dt_ch)).astype(BF16)
        s_ref[d] = state * jnp.exp(cum_end) + lax.dot_general(bmat, xw, TN_DIMS,
                                                              preferred_element_type=F32)

    def ctx_step(i, carry):
        chunk(n_lat + i, 0, False)
        chunk(n_all - 1 - i, 1, False)
        return carry

    def lat_step(i, carry):
        chunk(i, 0, True)
        chunk(n_lat - 1 - i, 1, True)
        return carry

    rows = 256
    for t0 in range(0, l, rows):
        y_ref[t0:t0 + rows, :] = dsk_ref[0] * xs_ref[0, t0:t0 + rows, :].astype(F32)
    s_ref[...] = jnp.zeros_like(s_ref)
    lax.fori_loop(0, n_ctx, ctx_step, 0)
    lax.fori_loop(0, n_lat, lat_step, 0)

    rows = 256
    for t0 in range(0, l, rows):
        y = y_ref[t0:t0 + rows, :] * _silu(z_ref[0, t0:t0 + rows, :].astype(F32))
        ms = jnp.mean(y * y, axis=-1, keepdims=True)
        o_ref[0, t0:t0 + rows, :] = (y * lax.rsqrt(ms + EPS) * nw_ref[0]).astype(o_ref.dtype)


SPLIT_K = 32


def _split3_cols(t3, b, s):
    g, hpg = SSD_GROUPS, SSD_HPG
    parts = jnp.concatenate([t3, jnp.zeros_like(t3[:, :1])], axis=1)
    parts = parts.reshape(b, SPLIT_K // hpg, s, 2, g, hpg).transpose(0, 4, 3, 2, 1, 5)
    return parts.reshape(b * g, 2, s, SPLIT_K)


def _expansion(width):
    k = jnp.arange(SPLIT_K)[:, None]
    n = jnp.arange(SSD_HPG * width)[None, :]
    return ((k % SSD_HPG == n // width) & (k < 3 * SSD_HPG)).astype(BF16)


def _ssd(xbc, proj, z_block0, cum3, dt3, cumr, dtr, dskip, norm_w, l, lc):
    b, s, _ = xbc.shape
    g = SSD_GROUPS
    gw = SSD_HPG * SSD_HEADDIM
    nchunks = s // SSD_CHUNK
    xs_blocks = g * gw // SSD_STATE
    e_seg = _expansion(SSD_CHUNK)
    e_ch = _expansion(SSD_HEADDIM)
    kern = functools.partial(_ssd_kernel, l=l, lc=lc)
    col_spec = pl.BlockSpec((1, 2, s, SPLIT_K), lambda i, j: (i * g + j, 0, 0, 0))
    row_spec = pl.BlockSpec((1, 2, nchunks, SSD_HPG, SSD_CHUNK), lambda i, j: (i * g + j, 0, 0, 0, 0))
    return pl.pallas_call(
        kern,
        grid=(b, g),
        in_specs=[pl.BlockSpec((1, s, gw), lambda i, j: (i, 0, j)),
                  pl.BlockSpec((1, s, SSD_STATE), lambda i, j: (i, 0, xs_blocks + j)),
                  pl.BlockSpec((1, s, SSD_STATE), lambda i, j: (i, 0, xs_blocks + g + j)),
                  pl.BlockSpec((1, l, gw), lambda i, j: (i, 0, z_block0 + j)),
                  col_spec, col_spec, row_spec, row_spec,
                  pl.BlockSpec(e_seg.shape, lambda i, j: (0, 0)),
                  pl.BlockSpec(e_ch.shape, lambda i, j: (0, 0)),
                  pl.BlockSpec((1, 1, gw), lambda i, j: (j, 0, 0)),
                  pl.BlockSpec((1, 1, gw), lambda i, j: (j, 0, 0))],
        out_specs=pl.BlockSpec((1, l, gw), lambda i, j: (i, 0, j)),
        out_shape=jax.ShapeDtypeStruct((b, l, g * gw), BF16),
        scratch_shapes=[pltpu.VMEM((2, SSD_STATE, gw), F32),
                        pltpu.VMEM((l, gw), F32)],
        compiler_params=_cparams("parallel", "parallel"),
        name="ssd_scan",
    )(xbc, xbc, xbc, proj, cum3, dt3, cumr, dtr, e_seg, e_ch, dskip, norm_w)


GLA_TILE = 256
GLA_TILES_PER_STEP = 2


def _split_dot(m, x):
    x_hi = x.astype(BF16)
    x_lo = (x - x_hi.astype(F32)).astype(BF16)
    return (jnp.dot(m, x_hi, preferred_element_type=F32) + jnp.dot(m, x_lo, preferred_element_type=F32))


def _split_dot3(a, b):
    a_hi = a.astype(BF16)
    a_lo = (a - a_hi.astype(F32)).astype(BF16)
    b_hi = b.astype(BF16)
    b_lo = (b - b_hi.astype(F32)).astype(BF16)
    return (jnp.dot(a_hi, b_hi, preferred_element_type=F32) + jnp.dot(a_lo, b_hi, preferred_element_type=F32)
            + jnp.dot(a_hi, b_lo, preferred_element_type=F32))


def _gla_kernel(q_ref, k_ref, v_ref, r_ref, lr_ref, w2f_ref, b2f_ref, w2b_ref, b2b_ref, nw_ref,
                o_ref, st_ref, y_ref, qd_ref, kt_ref, eb_ref, *, l, lc):
    c = GLA_CHUNK
    t = GLA_TILE
    n_lat, n_ctx = l // c, lc // c
    n_all = n_lat + n_ctx
    hk = q_ref.shape[2]
    scale = hk ** -0.5
    row = lax.broadcasted_iota(jnp.int32, (t, t), 0)
    colm = lax.broadcasted_iota(jnp.int32, (t, t), 1)
    same_chunk = (row // c) == (colm // c)
    masks = (same_chunk & (row >= colm), same_chunk & (row <= colm))

    def gates(off, emit, n_tiles):
        rows = n_tiles * t
        tok = pl.ds(off, rows)
        tiles = [slice(i * t, (i + 1) * t) for i in range(n_tiles)]
        per_tile = lambda f: jnp.concatenate([f(ts) for ts in tiles], axis=0)
        kf = k_ref[0, tok, :].astype(F32)
        lr_all = lr_ref[0, tok, :]
        for d in range(2):
            w2 = w2f_ref if d == 0 else w2b_ref
            b2 = b2f_ref if d == 0 else b2b_ref
            lr = lr_all[:, d * GLA_RANK:(d + 1) * GLA_RANK]
            logit = _split_dot3(lr, w2[...]) + b2[...]
            g = -_softplus(-logit) * (1.0 / GLA_NORMALIZER)
            tri = masks[d].astype(BF16)
            bcum = per_tile(lambda ts: _split_dot(tri, g[ts]))
            edge = c - 1 if d == 0 else 0
            btot = jnp.concatenate(
                [jnp.broadcast_to(bcum[j * c + edge:j * c + edge + 1, :], (c, hk)) for j in range(rows // c)],
                axis=0)
            kt_ref[d, tok, :] = (kf * jnp.exp(btot - bcum)).astype(BF16)
            ebt = jnp.exp(btot)
            for j in range(rows // c):
                eb_ref[d, pl.ds(off // c + j, 1)] = ebt[j * c:j * c + 8, :][None]
            if emit:
                qd = (q_ref[0, tok, :].astype(F32) * scale * jnp.exp(bcum)).astype(BF16)
                kd = (kf * jnp.exp(-bcum)).astype(BF16)
                qd_ref[d, tok, :] = qd

                def intra(ts):
                    a = lax.dot_general(qd[ts], kd[ts], NT_DIMS, preferred_element_type=F32)
                    a = jnp.where(masks[d], a, 0.0).astype(BF16)
                    return jnp.dot(a, v_ref[0, pl.ds(off + ts.start, t), :], preferred_element_type=F32)

                o = per_tile(intra)
                if d == 0:
                    y_ref[tok, :] = o
                else:
                    y_ref[tok, :] = y_ref[tok, :] + o

    def lat_gates(i, carry):
        gates(pl.multiple_of(i * (GLA_TILES_PER_STEP * t), GLA_TILES_PER_STEP * t), True, GLA_TILES_PER_STEP)
        return carry

    lax.fori_loop(0, l // (GLA_TILES_PER_STEP * t), lat_gates, 0)
    for i in range(lc // t):
        gates(l + i * t, False, 1)

    def step(ci, d, emit):
        tok = pl.ds(pl.multiple_of(ci * c, c), c)
        st = st_ref[d]
        if emit:
            o = lax.dot_general(qd_ref[d, tok, :], st.astype(BF16), NT_DIMS, preferred_element_type=F32)
            y_ref[tok, :] = y_ref[tok, :] + o
        st_ref[d] = st * eb_ref[d, ci][0:1, :] + lax.dot_general(
            v_ref[0, tok, :], kt_ref[d, tok, :], TN_DIMS, preferred_element_type=F32)

    def ctx_step(i, carry):
        step(n_lat + i, 0, False)
        step(n_all - 1 - i, 1, False)
        return carry

    def lat_step(i, carry):
        step(i, 0, True)
        step(n_lat - 1 - i, 1, True)
        return carry

    st_ref[...] = jnp.zeros_like(st_ref)
    lax.fori_loop(0, n_ctx, ctx_step, 0)
    lax.fori_loop(0, n_lat, lat_step, 0)

    rows = 256
    for t0 in range(0, l, rows):
        o = y_ref[t0:t0 + rows, :]
        ms = jnp.mean(o * o, axis=-1, keepdims=True)
        o = o * lax.rsqrt(ms + EPS) * nw_ref[...]
        o_ref[0, t0:t0 + rows, :] = (o * _silu(r_ref[0, t0:t0 + rows, :].astype(F32))).astype(o_ref.dtype)


def _gla(proj, small, w2f, b2f, w2b, b2b, norm_w, l, lc, blocks):
    b, s, _ = proj.shape
    h = GLA_HEADS
    hk = w2f.shape[1] // h
    hv = norm_w.shape[0]
    q0, k0, v0, r0 = blocks
    kern = functools.partial(_gla_kernel, l=l, lc=lc)
    return pl.pallas_call(
        kern,
        grid=(b, h),
        in_specs=[pl.BlockSpec((1, s, hk), lambda i, j: (i, 0, q0 // hk + j)),
                  pl.BlockSpec((1, s, hk), lambda i, j: (i, 0, k0 // hk + j)),
                  pl.BlockSpec((1, s, hv), lambda i, j: (i, 0, v0 // hv + j)),
                  pl.BlockSpec((1, l, hv), lambda i, j: (i, 0, r0 // hv + j)),
                  pl.BlockSpec((1, s, LANES), lambda i, j: (i, 0, 1)),
                  pl.BlockSpec((GLA_RANK, hk), lambda i, j: (0, j)),
                  pl.BlockSpec((1, hk), lambda i, j: (0, j)),
                  pl.BlockSpec((GLA_RANK, hk), lambda i, j: (0, j)),
                  pl.BlockSpec((1, hk), lambda i, j: (0, j)),
                  pl.BlockSpec((1, hv), lambda i, j: (0, 0))],
        out_specs=pl.BlockSpec((1, l, hv), lambda i, j: (i, 0, j)),
        out_shape=jax.ShapeDtypeStruct((b, l, h * hv), BF16),
        scratch_shapes=[pltpu.VMEM((2, hv, hk), F32),
                        pltpu.VMEM((l, hv), F32),
                        pltpu.VMEM((2, l, hk), BF16),
                        pltpu.VMEM((2, s, hk), BF16),
                        pltpu.VMEM((2, s // GLA_CHUNK, 8, hk), F32)],
        compiler_params=_cparams("parallel", "parallel"),
        name="gla_scan",
    )(proj, proj, proj, proj, small, w2f, b2f.reshape(1, -1), w2b, b2b.reshape(1, -1),
      norm_w.reshape(1, hv))


def _merge_kernel(a_ref, y_ref, wa_ref, wb_ref, gla_ref, glb_ref, ba_ref, bb_ref, o_ref):
    ya = jnp.dot(a_ref[0], wa_ref[...], preferred_element_type=F32)
    yb = jnp.dot(y_ref[0], wb_ref[...], preferred_element_type=F32)
    ga = jax.nn.sigmoid(gla_ref[0].astype(F32) + ba_ref[...])
    gb = jax.nn.sigmoid(glb_ref[0].astype(F32) + bb_ref[...])
    o_ref[0] = (ga * ya + gb * yb).astype(o_ref.dtype)


def _merge(a_n, y_n, w_a, w_b, proj, gl0, b_gate):
    b, l, ka = a_n.shape
    kb = y_n.shape[2]
    d = w_a.shape[1]
    bm = _pick(l, (1024, 512, 256))
    bn = 512
    nb = d // bn
    bg = b_gate.reshape(1, 2 * d)
    return pl.pallas_call(
        _merge_kernel,
        grid=(b, l // bm, nb),
        in_specs=[pl.BlockSpec((1, bm, ka), lambda i, t, j: (i, t, 0)),
                  pl.BlockSpec((1, bm, kb), lambda i, t, j: (i, t, 0)),
                  pl.BlockSpec((ka, bn), lambda i, t, j: (0, j)),
                  pl.BlockSpec((kb, bn), lambda i, t, j: (0, j)),
                  pl.BlockSpec((1, bm, bn), lambda i, t, j: (i, t, gl0 // bn + j)),
                  pl.BlockSpec((1, bm, bn), lambda i, t, j: (i, t, gl0 // bn + nb + j)),
                  pl.BlockSpec((1, bn), lambda i, t, j: (0, j)),
                  pl.BlockSpec((1, bn), lambda i, t, j: (0, nb + j))],
        out_specs=pl.BlockSpec((1, bm, bn), lambda i, t, j: (i, t, j)),
        out_shape=jax.ShapeDtypeStruct((b, l, d), BF16),
        compiler_params=_cparams("parallel", "parallel", "parallel"),
        name="branch_merge",
    )(a_n, y_n, w_a, w_b, proj, proj, bg, bg)


def _resid_kernel(m_ref, w_ref, x_ref, g_ref, o_ref):
    y = jnp.dot(m_ref[0], w_ref[...], preferred_element_type=F32)
    o_ref[0] = x_ref[0] + g_ref[0] * y


def _resid(m, w_o, x, gate):
    b, l, d = x.shape
    bm = _pick(l, (1024, 512, 256))
    bn = 512
    return pl.pallas_call(
        _resid_kernel,
        grid=(b, l // bm, d // bn),
        in_specs=[pl.BlockSpec((1, bm, d), lambda i, t, j: (i, t, 0)),
                  pl.BlockSpec((d, bn), lambda i, t, j: (0, j)),
                  pl.BlockSpec((1, bm, bn), lambda i, t, j: (i, t, j)),
                  pl.BlockSpec((1, 1, bn), lambda i, t, j: (i, 0, j))],
        out_specs=pl.BlockSpec((1, bm, bn), lambda i, t, j: (i, t, j)),
        out_shape=jax.ShapeDtypeStruct((b, l, d), F32),
        compiler_params=_cparams("parallel", "parallel", "parallel"),
        name="attn_resid",
    )(m, w_o, x, gate)


def _peer_q_kernel(x_ref, nw_ref, sh_ref, sc_ref, wq_ref, h_ref, q_ref, lhs_ref):
    @pl.when(pl.program_id(2) == 0)
    def _():
        xf = x_ref[0]
        ms = jnp.mean(xf * xf, axis=-1, keepdims=True)
        y = xf * lax.rsqrt(ms + EPS) * nw_ref[...]
        hb = (y * (1.0 + sc_ref[0]) + sh_ref[0]).astype(BF16)
        lhs_ref[...] = hb
        h_ref[0] = hb

    q_ref[0] = jnp.dot(lhs_ref[...], wq_ref[...], preferred_element_type=F32)


def _peer_q(x1, norm_w, shift, scale, wq):
    b, l, d = x1.shape
    n = wq.shape[1]
    bm = _pick(l, (512, 256))
    bn = n
    return pl.pallas_call(
        _peer_q_kernel,
        grid=(b, l // bm, n // bn),
        in_specs=[pl.BlockSpec((1, bm, d), lambda i, t, j: (i, t, 0)),
                  pl.BlockSpec((1, d), lambda i, t, j: (0, 0)),
                  pl.BlockSpec((1, 1, d), lambda i, t, j: (i, 0, 0)),
                  pl.BlockSpec((1, 1, d), lambda i, t, j: (i, 0, 0)),
                  pl.BlockSpec((d, bn), lambda i, t, j: (0, j))],
        out_specs=[pl.BlockSpec((1, bm, d), lambda i, t, j: (i, t, 0)),
                   pl.BlockSpec((1, bm, bn), lambda i, t, j: (i, t, j))],
        out_shape=[jax.ShapeDtypeStruct((b, l, d), BF16),
                   jax.ShapeDtypeStruct((b, l, n), F32)],
        scratch_shapes=[pltpu.VMEM((bm, d), BF16)],
        compiler_params=_cparams("parallel", "parallel", "arbitrary"),
        name="peer_query",
    )(x1, norm_w.reshape(1, d), shift, scale, wq)


def _top_values(x, k, rows):
    sub = lax.broadcasted_iota(jnp.int32, (rows, x.shape[1]), 0)
    out = jnp.full((rows, x.shape[1]), -jnp.inf, F32)
    for i in range(k):
        m = jnp.max(x, axis=0, keepdims=True)
        out = jnp.where(sub == i, m, out)
        x = jnp.where(x == m, -jnp.inf, x)
    return out


def _peer_score_kernel(q_ref, keys_ref, c1_ref, e1_ref, s2_ref, e2_ref):
    k = PEER_TOPK
    dk = keys_ref.shape[3]
    q = q_ref[...]
    s1 = lax.dot_general(keys_ref[0, 0], q[:, 0:dk], NT_DIMS, precision=HIGHEST,
                         preferred_element_type=F32)
    s2 = lax.dot_general(keys_ref[0, 1], q[:, dk:2 * dk], NT_DIMS, precision=HIGHEST,
                         preferred_element_type=F32)
    n = k + 1
    pad = -(-n // 8) * 8
    sv1 = _top_values(s1, n, pad)
    sv2 = _top_values(s2, n, pad)
    assert n // 2 <= 8 and n // 9 == 1
    cand = jnp.concatenate(
        [sv1[0:1, :] + sv2]
        + [sv1[a:a + 1, :] + sv2[0:8, :] for a in range(1, 8)]
        + [sv1[8:pad, :] + sv2[0:1, :]], axis=0)
    cv = _top_values(cand, n, pad)
    z = jnp.sum(jnp.exp(cv[0:k, :] - cv[0:1, :]), axis=0, keepdims=True)
    thr = 0.5 * (cv[k - 1:k, :] + cv[k:k + 1, :])
    c1_ref[0] = thr - s1
    s2_ref[0] = s2
    e1_ref[0] = jnp.exp(s1 - sv1[0:1, :]) / z
    e2_ref[0] = jnp.exp(s2 - sv2[0:1, :])


def _peer_scores(q, keys):
    t, n = q.shape
    h, _, nk, dk = keys.shape
    tt = _pick(t, (512, 256))
    big = jax.ShapeDtypeStruct((h, nk, t), F32)
    big_spec = pl.BlockSpec((1, nk, tt), lambda i, j: (j, 0, i))
    return pl.pallas_call(
        _peer_score_kernel,
        grid=(t // tt, h),
        in_specs=[pl.BlockSpec((tt, 2 * dk), lambda i, j: (i, j)),
                  pl.BlockSpec((1, 2, nk, dk), lambda i, j: (j, 0, 0, 0))],
        out_specs=[big_spec, big_spec, big_spec, big_spec],
        out_shape=[big, big, big, big],
        compiler_params=_cparams("parallel", "parallel"),
        name="peer_scores",
    )(q, keys)


PEER_EXPERT_BLOCK = 1024
GATE_ROWS = 8


def _peer_expert_kernel(h_ref, u_ref, vt_ref, c1_ref, e1_ref, s2_ref, e2_ref,
                        o_ref, ht_ref, w_ref, c8_ref, e8_ref, *, n_sub):
    j = pl.program_id(1)
    nh, nk, bm = s2_ref.shape
    sub = 8

    @pl.when(j == 0)
    def _():
        o_ref[...] = jnp.zeros_like(o_ref)
        ht_ref[...] = h_ref[...].astype(F32).T.astype(BF16)

    i1_0 = pl.multiple_of(j * n_sub, n_sub)
    for hd in range(nh):
        cgrp = c1_ref[hd, pl.ds(i1_0, n_sub), :]
        egrp = e1_ref[hd, pl.ds(i1_0, n_sub), :]
        for sb in range(n_sub):
            r0 = (hd * n_sub + sb) * sub
            c8_ref[r0:r0 + sub, :] = jnp.broadcast_to(cgrp[sb:sb + 1, :], (sub, bm))
            e8_ref[r0:r0 + sub, :] = jnp.broadcast_to(egrp[sb:sb + 1, :], (sub, bm))

    act = jnp.dot(u_ref[...], ht_ref[...], preferred_element_type=F32)

    n_part = GATE_ROWS // sub

    for lc in range(bm // LANES):
        ls = slice(lc * LANES, (lc + 1) * LANES)
        for rc in range(nk // GATE_ROWS):
            base = rc * GATE_ROWS
            accs = [[jnp.zeros((sub, LANES), F32) for _ in range(n_part)] for _ in range(n_sub)]
            for hd in range(nh):
                s2v = [s2_ref[hd, base + k * sub:base + (k + 1) * sub, ls] for k in range(n_part)]
                e2v = [e2_ref[hd, base + k * sub:base + (k + 1) * sub, ls] for k in range(n_part)]
                for sb in range(n_sub):
                    r0 = (hd * n_sub + sb) * sub
                    cut = c8_ref[r0:r0 + sub, ls]
                    e1v = e8_ref[r0:r0 + sub, ls]
                    for k in range(n_part):
                        accs[sb][k] = accs[sb][k] + jnp.where(s2v[k] >= cut, e2v[k], 0.0) * e1v
            for sb in range(n_sub):
                for k in range(n_part):
                    r0 = sb * nk + base + k * sub
                    w_ref[r0:r0 + sub, ls] = accs[sb][k]

    gelu = 0.5 * act * (1.0 + lax.erf(act * (2.0 ** -0.5)))
    pmat = (gelu * w_ref[...]).astype(BF16)
    o_ref[...] += jnp.dot(vt_ref[...], pmat, preferred_element_type=F32)


def _peer_experts(h2, u, vt, c1, e1, s2, e2):
    t, d = h2.shape
    e = u.shape[0]
    nh, nk, _ = s2.shape
    bm = _pick(t, (512, 256))
    eb = PEER_EXPERT_BLOCK
    n_sub = eb // nk
    kern = functools.partial(_peer_expert_kernel, n_sub=n_sub)
    big_spec = pl.BlockSpec((nh, nk, bm), lambda i, j: (0, 0, i))
    return pl.pallas_call(
        kern,
        grid=(t // bm, e // eb),
        in_specs=[pl.BlockSpec((bm, d), lambda i, j: (i, 0)),
                  pl.BlockSpec((eb, d), lambda i, j: (j, 0)),
                  pl.BlockSpec((d, eb), lambda i, j: (0, j)),
                  big_spec, big_spec, big_spec, big_spec],
        out_specs=pl.BlockSpec((d, bm), lambda i, j: (0, i)),
        out_shape=jax.ShapeDtypeStruct((d, t), F32),
        scratch_shapes=[pltpu.VMEM((d, bm), BF16),
                        pltpu.VMEM((eb, bm), F32),
                        pltpu.VMEM((nh * n_sub * 8, bm), F32),
                        pltpu.VMEM((nh * n_sub * 8, bm), F32)],
        compiler_params=_cparams("parallel", "arbitrary"),
        name="peer_experts",
    )(h2, u, vt, c1, e1, s2, e2)


def _final_kernel(x_ref, yt_ref, g_ref, w_ref, o_ref):
    xf = x_ref[0] + g_ref[0] * yt_ref[...].T
    ms = jnp.mean(xf * xf, axis=-1, keepdims=True)
    o_ref[0] = xf * lax.rsqrt(ms + EPS) * w_ref[...]


def _final(x1, yt, gate, w):
    b, l, d = x1.shape
    bm = _pick(l, (512, 256))
    nt = l // bm
    return pl.pallas_call(
        _final_kernel,
        grid=(b, l // bm),
        in_specs=[pl.BlockSpec((1, bm, d), lambda i, t: (i, t, 0)),
                  pl.BlockSpec((d, bm), lambda i, t: (0, i * nt + t)),
                  pl.BlockSpec((1, 1, d), lambda i, t: (i, 0, 0)),
                  pl.BlockSpec((1, d), lambda i, t: (0, 0))],
        out_specs=pl.BlockSpec((1, bm, d), lambda i, t: (i, t, 0)),
        out_shape=jax.ShapeDtypeStruct((b, l, d), F32),
        compiler_params=_cparams("parallel", "parallel"),
        name="final_norm",
    )(x1, yt, gate, w.reshape(1, d))


def _layer(x, ctx, mod_x, mod_c, p):
    b, l, d = x.shape
    lc = ctx.shape[1]
    s = l + lc
    dk = p['w_lr2_f'].shape[1]
    dv = p['w_gla_out'].shape[0]
    di = p['w_ssd_out'].shape[0]
    bc = SSD_GROUPS * SSD_STATE
    n_ssd_heads = p['a_log_f'].shape[0]

    sizes = (dk, dk, dv, dv, GLA_RANK, GLA_RANK, di, di + 2 * bc, n_ssd_heads, n_ssd_heads, 2 * d)
    offs = [0]
    for sz in sizes:
        offs.append(offs[-1] + sz)
    w_in = p['w_in']
    seg = lambda i: w_in[:, offs[i]:offs[i + 1]]
    w_main = jnp.concatenate([seg(0), seg(1), seg(2), seg(3), seg(6), seg(7), seg(10)], axis=1).astype(BF16)
    n_small = 2 * LANES
    w_small = jnp.concatenate(
        [seg(8), seg(9), seg(4), seg(5),
         jnp.zeros((d, n_small - 2 * n_ssd_heads - 2 * GLA_RANK), w_in.dtype)], axis=1).astype(BF16)
    q0, k0, v0, r0 = 0, dk, 2 * dk, 2 * dk + dv
    z0 = r0 + dv
    xbc0 = z0 + di
    gl0 = xbc0 + di + 2 * bc

    shift1 = jnp.stack([mod_x[0], jnp.broadcast_to(mod_c[0], (b, d))], axis=1).reshape(b, 2, 1, d)
    scale1 = jnp.stack([mod_x[1], jnp.broadcast_to(mod_c[1], (b, d))], axis=1).reshape(b, 2, 1, d)
    h = _norm_mod(x, ctx, p['norm1_w'], shift1, scale1)

    h2d = h.reshape(b * s, d)
    proj = _matmul(h2d, w_main, BF16, "in_proj").reshape(b, s, -1)
    small = _matmul(h2d, w_small, F32, "in_proj_gates").reshape(b, s, n_small)

    xbc = _conv(proj, xbc0 // 512, di + 2 * bc, p['conv_w'], p['conv_b'], l, lc)
    bias = jnp.concatenate([p['dt_bias_f'], p['dt_bias_b']]).reshape(1, LANES).astype(F32)
    a_neg = -jnp.exp(jnp.concatenate([p['a_log_f'], p['a_log_b']]).astype(F32)).reshape(1, LANES)
    dt, cum, dt3, cum3 = _ssd_prep(small, bias, a_neg)
    g, hpg, c = SSD_GROUPS, SSD_HPG, SSD_CHUNK
    to_rows = lambda t: t.reshape(b, s // c, c, 2, g, hpg).transpose(0, 4, 3, 1, 5, 2).reshape(
        b * g, 2, s // c, hpg, c)
    dskip = jnp.repeat(p['d_skip'].astype(F32), SSD_HEADDIM).reshape(g, 1, hpg * SSD_HEADDIM)
    ssd_nw = p['ssd_norm_w'].astype(F32).reshape(g, 1, hpg * SSD_HEADDIM)
    y_n = _ssd(xbc, proj, z0 // 512, _split3_cols(cum3, b, s), _split3_cols(dt3, b, s),
               to_rows(cum), to_rows(dt), dskip, ssd_nw, l, lc)

    a_n = _gla(proj, small, p['w_lr2_f'], p['b_lr_f'], p['w_lr2_b'], p['b_lr_b'], p['gla_norm_w'],
               l, lc, (q0, k0, v0, r0))

    m = _merge(a_n, y_n, p['w_gla_out'].astype(BF16), p['w_ssd_out'].astype(BF16), proj, gl0,
               p['b_gate'])
    return _resid(m, p['w_o'].astype(BF16), x, mod_x[2].reshape(b, 1, d))


def _peer(x1, mod_x, norm_w, wq, keys, u, v):
    b, l, d = x1.shape
    h2, q = _peer_q(x1, norm_w, mod_x[3].reshape(b, 1, d), mod_x[4].reshape(b, 1, d), wq.astype(BF16))
    c1, e1, s2, e2 = _peer_scores(q.reshape(b * l, -1), keys)
    return _peer_experts(h2.reshape(b * l, d), u.astype(BF16), v.astype(BF16).T, c1, e1, s2, e2)


def kernel(x, c, ctx, c_ctx, w_ada, b_ada, norm1_w, w_in, b_gate, w_lr2_f, b_lr_f, w_lr2_b, b_lr_b,
           gla_norm_w, w_gla_out, conv_w, conv_b, a_log_f, a_log_b, dt_bias_f, dt_bias_b, d_skip,
           ssd_norm_w, w_ssd_out, w_o, norm2_w, peer_wq, peer_keys, peer_u, peer_v, final_norm_w):
    b, l, d = x.shape
    depth = w_in.shape[0]
    assert depth == 1, "context-stream update for deeper stacks is not implemented"
    layer = 0
    rows = -(-(b + 1) // 8) * 8
    c_all = jnp.concatenate([c, c_ctx[None, :], jnp.zeros((rows - b - 1, d), c.dtype)], axis=0)
    mod = _ada(c_all, w_ada[layer], b_ada[layer])
    mod_x = [mod[:b, i * d:(i + 1) * d] for i in range(N_MOD)]
    mod_c = [mod[b, i * d:(i + 1) * d] for i in range(N_MOD)]
    p = {
        'norm1_w': norm1_w[layer], 'w_in': w_in[layer], 'b_gate': b_gate[layer],
        'w_lr2_f': w_lr2_f[layer], 'b_lr_f': b_lr_f[layer],
        'w_lr2_b': w_lr2_b[layer], 'b_lr_b': b_lr_b[layer],
        'gla_norm_w': gla_norm_w[layer], 'w_gla_out': w_gla_out[layer],
        'conv_w': conv_w[layer], 'conv_b': conv_b[layer],
        'a_log_f': a_log_f[layer], 'a_log_b': a_log_b[layer],
        'dt_bias_f': dt_bias_f[layer], 'dt_bias_b': dt_bias_b[layer],
        'd_skip': d_skip[layer], 'ssd_norm_w': ssd_norm_w[layer],
        'w_ssd_out': w_ssd_out[layer], 'w_o': w_o[layer],
    }
    x1 = _layer(x, ctx, mod_x, mod_c, p)
    y = _peer(x1, mod_x, norm2_w[layer], peer_wq[layer], peer_keys[layer], peer_u[layer], peer_v[layer])
    return _final(x1, y, mod_x[5].reshape(b, 1, d), final_norm_w)
```

```python
import functools

import jax
import jax.numpy as jnp
from jax import lax
from jax.experimental import pallas as pl
from jax.experimental.pallas import tpu as pltpu

F32 = jnp.float32
BF16 = jnp.bfloat16
EPS = 1e-6
HIGHEST = lax.Precision.HIGHEST

N_MOD = 6
GRID_W = 64
GLA_HEADS = 4
GLA_RANK = 16
GLA_NORMALIZER = 16.0
GLA_CHUNK = 64
SSD_HEADDIM = 64
SSD_GROUPS = 8
SSD_HPG = 8
SSD_STATE = 128
SSD_CHUNK = 128
PEER_HEADS = 8
PEER_NKEYS = 128
PEER_TOPK = 16

VMEM_LIMIT_BYTES = 56 * 1024 * 1024
LANES = 128

NT_DIMS = (((1,), (1,)), ((), ()))
TN_DIMS = (((0,), (0,)), ((), ()))


def _cparams(*sem, flags=None):
    return pltpu.CompilerParams(dimension_semantics=sem, vmem_limit_bytes=VMEM_LIMIT_BYTES, flags=flags)


def _pick(n, options):
    for o in options:
        if n % o == 0:
            return o
    raise ValueError(f"no tile in {options} divides {n}")


def _softplus(x):
    return jnp.maximum(x, 0.0) + jnp.log1p(jnp.exp(-jnp.abs(x)))


def _silu(x):
    return x * jax.nn.sigmoid(x)


def _ada_kernel(c_ref, w_ref, b_ref, o_ref):
    a = _silu(c_ref[...])
    o_ref[...] = jnp.dot(a, w_ref[...], precision=HIGHEST, preferred_element_type=F32) + b_ref[...]


def _ada(c_all, w_ada, b_ada):
    m, d = c_all.shape
    n = w_ada.shape[1]
    bn = _pick(n, (1024, 512, 256, 128))
    return pl.pallas_call(
        _ada_kernel,
        grid=(n // bn,),
        in_specs=[pl.BlockSpec((m, d), lambda j: (0, 0)),
                  pl.BlockSpec((d, bn), lambda j: (0, j)),
                  pl.BlockSpec((1, bn), lambda j: (0, j))],
        out_specs=pl.BlockSpec((m, bn), lambda j: (0, j)),
        out_shape=jax.ShapeDtypeStruct((m, n), F32),
        compiler_params=_cparams("parallel"),
        name="ada_mod",
    )(c_all, w_ada, b_ada.reshape(1, n))


def _norm_mod_kernel(x_ref, ctx_ref, w_ref, sh_ref, sc_ref, o_ref, *, n_lat_tiles):
    j = pl.program_id(1)

    def emit(src):
        xf = src[0]
        ms = jnp.mean(xf * xf, axis=-1, keepdims=True)
        y = xf * lax.rsqrt(ms + EPS) * w_ref[...]
        o_ref[0] = (y * (1.0 + sc_ref[0, 0]) + sh_ref[0, 0]).astype(o_ref.dtype)

    @pl.when(j < n_lat_tiles)
    def _():
        emit(x_ref)

    @pl.when(j >= n_lat_tiles)
    def _():
        emit(ctx_ref)


def _norm_mod(x, ctx, w, shift, scale):
    b, l, d = x.shape
    lc = ctx.shape[1]
    tn = _pick(lc, (256, 128))
    nl, nc = l // tn, lc // tn
    kern = functools.partial(_norm_mod_kernel, n_lat_tiles=nl)
    return pl.pallas_call(
        kern,
        grid=(b, nl + nc),
        in_specs=[pl.BlockSpec((1, tn, d), lambda i, j: (i, jnp.minimum(j, nl - 1), 0)),
                  pl.BlockSpec((1, tn, d), lambda i, j: (i, jnp.maximum(j - nl, 0), 0)),
                  pl.BlockSpec((1, d), lambda i, j: (0, 0)),
                  pl.BlockSpec((1, 1, 1, d), lambda i, j: (i, j // nl, 0, 0)),
                  pl.BlockSpec((1, 1, 1, d), lambda i, j: (i, j // nl, 0, 0))],
        out_specs=pl.BlockSpec((1, tn, d), lambda i, j: (i, j, 0)),
        out_shape=jax.ShapeDtypeStruct((b, l + lc, d), BF16),
        compiler_params=_cparams("parallel", "parallel"),
        name="norm_mod",
    )(x, ctx, w.reshape(1, d), shift, scale)


def _matmul_kernel(a_ref, b_ref, o_ref):
    o_ref[...] = jnp.dot(a_ref[...], b_ref[...], preferred_element_type=F32).astype(o_ref.dtype)


def _matmul(a, b, out_dtype, name):
    m, k = a.shape
    n = b.shape[1]
    bm = _pick(m, (2048, 1024, 512, 256))
    bn = _pick(n, (1024, 512, 256))
    return pl.pallas_call(
        _matmul_kernel,
        grid=(m // bm, n // bn),
        in_specs=[pl.BlockSpec((bm, k), lambda i, j: (i, 0)),
                  pl.BlockSpec((k, bn), lambda i, j: (0, j))],
        out_specs=pl.BlockSpec((bm, bn), lambda i, j: (i, j)),
        out_shape=jax.ShapeDtypeStruct((m, n), out_dtype),
        compiler_params=_cparams("parallel", "parallel"),
        name=name,
    )(a, b)


CONV_TOK = 256
CTX_HALO = 8
LAT_HALO = GRID_W + 8


def _conv_kernel(x_ref, w_ref, b_ref, o_ref, lat_ref, ctx_ref, *, l, lc):
    tc = x_ref.shape[2]
    lat_ref[0:LAT_HALO, :] = jnp.zeros((LAT_HALO, tc), F32)
    lat_ref[LAT_HALO + l:LAT_HALO + l + LAT_HALO, :] = jnp.zeros((LAT_HALO, tc), F32)
    ctx_ref[0:CTX_HALO, :] = jnp.zeros((CTX_HALO, tc), F32)
    ctx_ref[CTX_HALO + lc:CTX_HALO + lc + CTX_HALO, :] = jnp.zeros((CTX_HALO, tc), F32)
    lat_ref[LAT_HALO:LAT_HALO + l, :] = x_ref[0, 0:l, :].astype(F32)
    ctx_ref[CTX_HALO:CTX_HALO + lc, :] = x_ref[0, l:l + lc, :].astype(F32)

    bias = b_ref[...]
    col = lax.broadcasted_iota(jnp.int32, (CONV_TOK, tc), 0) % GRID_W
    not_first = col != 0
    not_last = col != GRID_W - 1

    for t0 in range(0, l, CONV_TOK):
        acc = jnp.zeros((CONV_TOK, tc), F32)
        for kw in range(3):
            part = jnp.zeros((CONV_TOK, tc), F32)
            for kh in range(3):
                start = LAT_HALO + t0 + GRID_W * (kh - 1) + (kw - 1)
                part = part + w_ref[kh * 3 + kw:kh * 3 + kw + 1, :] * lat_ref[start:start + CONV_TOK, :]
            if kw == 0:
                part = jnp.where(not_first, part, 0.0)
            elif kw == 2:
                part = jnp.where(not_last, part, 0.0)
            acc = acc + part
        o_ref[0, t0:t0 + CONV_TOK, :] = _silu(acc + bias).astype(o_ref.dtype)

    ctx_tok = min(CONV_TOK, lc)
    for t0 in range(0, lc, ctx_tok):
        acc = jnp.zeros((ctx_tok, tc), F32)
        for kw in range(3):
            start = CTX_HALO + t0 + (kw - 1)
            acc = acc + w_ref[3 + kw:4 + kw, :] * ctx_ref[start:start + ctx_tok, :]
        o_ref[0, l + t0:l + t0 + ctx_tok, :] = _silu(acc + bias).astype(o_ref.dtype)


def _conv(proj, col_block0, n_ch, conv_w, conv_b, l, lc):
    b, s, _ = proj.shape
    tc = 512
    kern = functools.partial(_conv_kernel, l=l, lc=lc)
    return pl.pallas_call(
        kern,
        grid=(b, n_ch // tc),
        in_specs=[pl.BlockSpec((1, s, tc), lambda i, j: (i, 0, col_block0 + j)),
                  pl.BlockSpec((9, tc), lambda i, j: (0, j)),
                  pl.BlockSpec((1, tc), lambda i, j: (0, j))],
        out_specs=pl.BlockSpec((1, s, tc), lambda i, j: (i, 0, j)),
        out_shape=jax.ShapeDtypeStruct((b, s, n_ch), BF16),
        scratch_shapes=[pltpu.VMEM((l + 2 * LAT_HALO, tc), F32),
                        pltpu.VMEM((lc + 2 * CTX_HALO, tc), F32)],
        compiler_params=_cparams("parallel", "parallel"),
        name="dwconv_silu",
    )(proj, conv_w.reshape(9, n_ch), conv_b.reshape(1, n_ch))


def _bf16_terms(x):
    hi = x.astype(BF16)
    r1 = x - hi.astype(F32)
    mid = r1.astype(BF16)
    lo = (r1 - mid.astype(F32)).astype(BF16)
    return hi, mid, lo


def _ssd_prep_kernel(x_ref, bias_ref, a_ref, dtr_ref, cumr_ref, dt3_ref, cum3_ref):
    c = SSD_CHUNK
    s = x_ref.shape[1]
    n_dg = LANES // SSD_HPG
    row = lax.broadcasted_iota(jnp.int32, (c, c), 0)
    colm = lax.broadcasted_iota(jnp.int32, (c, c), 1)
    tri_f = (row >= colm).astype(F32)
    tri_b = (row <= colm).astype(F32)
    is_fwd = lax.broadcasted_iota(jnp.int32, (c, LANES), 1) < (LANES // 2)
    for t0 in range(0, s, c):
        dt = _softplus(x_ref[0, t0:t0 + c, :] + bias_ref[...])
        dta = dt * a_ref[...]
        cf = jnp.dot(tri_f, dta, precision=HIGHEST, preferred_element_type=F32)
        cb = jnp.dot(tri_b, dta, precision=HIGHEST, preferred_element_type=F32)
        cum = jnp.where(is_fwd, cf, cb)
        dt_t, cum_t = dt.T, cum.T
        for dg in range(n_dg):
            dtr_ref[0, dg, t0 // c] = dt_t[dg * SSD_HPG:(dg + 1) * SSD_HPG, :]
            cumr_ref[0, dg, t0 // c] = cum_t[dg * SSD_HPG:(dg + 1) * SSD_HPG, :]
        for term, (dt_t, cum_t) in enumerate(zip(_bf16_terms(dt), _bf16_terms(cum))):
            dt3_ref[0, term, t0:t0 + c, :] = dt_t
            cum3_ref[0, term, t0:t0 + c, :] = cum_t


def _ssd_prep(small, bias, a):
    b, s, _ = small.shape
    n_dg = LANES // SSD_HPG
    rows = jax.ShapeDtypeStruct((b, n_dg, s // SSD_CHUNK, SSD_HPG, SSD_CHUNK), F32)
    out3 = jax.ShapeDtypeStruct((b, 3, s, LANES), BF16)
    spec = pl.BlockSpec((1, s, LANES), lambda i: (i, 0, 0))
    rows_spec = pl.BlockSpec((1, n_dg, s // SSD_CHUNK, SSD_HPG, SSD_CHUNK), lambda i: (i, 0, 0, 0, 0))
    spec3 = pl.BlockSpec((1, 3, s, LANES), lambda i: (i, 0, 0, 0))
    return pl.pallas_call(
        _ssd_prep_kernel,
        grid=(b,),
        in_specs=[spec,
                  pl.BlockSpec((1, LANES), lambda i: (0, 0)),
                  pl.BlockSpec((1, LANES), lambda i: (0, 0))],
        out_specs=[rows_spec, rows_spec, spec3, spec3],
        out_shape=[rows, rows, out3, out3],
        compiler_params=_cparams("parallel"),
        name="ssd_prep",
    )(small, bias, a)


def _ssd_kernel(xs_ref, bm_ref, cm_ref, z_ref, cum3_ref, dt3_ref, cumr_ref, dtr_ref, e_seg_ref, e_ch_ref,
                dsk_ref, nw_ref, o_ref, s_ref, y_ref, *, l, lc):
    c = SSD_CHUNK
    p = SSD_HEADDIM
    nh = SSD_HPG
    n_lat, n_ctx = l // c, lc // c
    n_all = n_lat + n_ctx
    row = lax.broadcasted_iota(jnp.int32, (c, c), 0)
    colm = lax.broadcasted_iota(jnp.int32, (c, c), 1)
    masks = (row >= colm, row <= colm)

    def chunk(ci, d, emit):
        tok = pl.ds(pl.multiple_of(ci * c, c), c)
        last = c - 1 if d == 0 else 0
        x = xs_ref[0, tok, :]
        bmat = bm_ref[0, tok, :]
        cum3 = cum3_ref[0, d, tok, :]
        cum_ch = jnp.dot(cum3, e_ch_ref[...], preferred_element_type=F32)
        dt_ch = jnp.dot(dt3_ref[0, d, tok, :], e_ch_ref[...], preferred_element_type=F32)
        cum_end = cum_ch[last:last + 1, :]
        state = s_ref[d]
        if emit:
            cmat = cm_ref[0, tok, :]
            cumr = cumr_ref[0, d, 0, ci]
            dtr = dtr_ref[0, d, 0, ci]
            cum_seg = jnp.dot(cum3, e_seg_ref[...], preferred_element_type=F32)
            cb = lax.dot_general(cmat, bmat, NT_DIMS, preferred_element_type=F32)
            ys = []
            for h in range(nh):
                seg = cum_seg[:, h * c:(h + 1) * c] - cumr[h:h + 1, :]
                decay = jnp.exp(jnp.where(masks[d], seg, -jnp.inf))
                m = (cb * decay * dtr[h:h + 1, :]).astype(BF16)
                ys.append(jnp.dot(m, x[:, h * p:(h + 1) * p], preferred_element_type=F32))
            y = jnp.concatenate(ys, axis=1)
            y = y + jnp.dot(cmat, state.astype(BF16), preferred_element_type=F32) * jnp.exp(cum_ch)
            y_ref[tok, :] = y_ref[tok, :] + y
        xw = (x.astype(F32) * (jnp.exp(cum_end - cum_ch) * dt_ch)).astype(BF16)
        s_ref[d] = state * jnp.exp(cum_end) + lax.dot_general(bmat, xw, TN_DIMS,
                                                              preferred_element_type=F32)

    def ctx_step(i, carry):
        chunk(n_lat + i, 0, False)
        chunk(n_all - 1 - i, 1, False)
        return carry

    def lat_step(i, carry):
        chunk(i, 0, True)
        chunk(n_lat - 1 - i, 1, True)
        return carry

    rows = 256
    for t0 in range(0, l, rows):
        y_ref[t0:t0 + rows, :] = dsk_ref[0] * xs_ref[0, t0:t0 + rows, :].astype(F32)
    s_ref[...] = jnp.zeros_like(s_ref)
    lax.fori_loop(0, n_ctx, ctx_step, 0)
    lax.fori_loop(0, n_lat, lat_step, 0)

    rows = 256
    for t0 in range(0, l, rows):
        y = y_ref[t0:t0 + rows, :] * _silu(z_ref[0, t0:t0 + rows, :].astype(F32))
        ms = jnp.mean(y * y, axis=-1, keepdims=True)
        o_ref[0, t0:t0 + rows, :] = (y * lax.rsqrt(ms + EPS) * nw_ref[0]).astype(o_ref.dtype)


SPLIT_K = 32


def _split3_cols(t3, b, s):
    g, hpg = SSD_GROUPS, SSD_HPG
    parts = jnp.concatenate([t3, jnp.zeros_like(t3[:, :1])], axis=1)
    parts = parts.reshape(b, SPLIT_K // hpg, s, 2, g, hpg).transpose(0, 4, 3, 2, 1, 5)
    return parts.reshape(b * g, 2, s, SPLIT_K)


def _expansion(width):
    k = jnp.arange(SPLIT_K)[:, None]
    n = jnp.arange(SSD_HPG * width)[None, :]
    return ((k % SSD_HPG == n // width) & (k < 3 * SSD_HPG)).astype(BF16)


def _ssd(xbc, proj, z_block0, cum3, dt3, cumr, dtr, dskip, norm_w, l, lc):
    b, s, _ = xbc.shape
    g = SSD_GROUPS
    gw = SSD_HPG * SSD_HEADDIM
    nchunks = s // SSD_CHUNK
    xs_blocks = g * gw // SSD_STATE
    e_seg = _expansion(SSD_CHUNK)
    e_ch = _expansion(SSD_HEADDIM)
    kern = functools.partial(_ssd_kernel, l=l, lc=lc)
    col_spec = pl.BlockSpec((1, 2, s, SPLIT_K), lambda i, j: (i * g + j, 0, 0, 0))
    row_spec = pl.BlockSpec((1, 2, 1, nchunks, SSD_HPG, SSD_CHUNK), lambda i, j: (i, 0, j, 0, 0, 0))
    return pl.pallas_call(
        kern,
        grid=(b, g),
        in_specs=[pl.BlockSpec((1, s, gw), lambda i, j: (i, 0, j)),
                  pl.BlockSpec((1, s, SSD_STATE), lambda i, j: (i, 0, xs_blocks + j)),
                  pl.BlockSpec((1, s, SSD_STATE), lambda i, j: (i, 0, xs_blocks + g + j)),
                  pl.BlockSpec((1, l, gw), lambda i, j: (i, 0, z_block0 + j)),
                  col_spec, col_spec, row_spec, row_spec,
                  pl.BlockSpec(e_seg.shape, lambda i, j: (0, 0)),
                  pl.BlockSpec(e_ch.shape, lambda i, j: (0, 0)),
                  pl.BlockSpec((1, 1, gw), lambda i, j: (j, 0, 0)),
                  pl.BlockSpec((1, 1, gw), lambda i, j: (j, 0, 0))],
        out_specs=pl.BlockSpec((1, l, gw), lambda i, j: (i, 0, j)),
        out_shape=jax.ShapeDtypeStruct((b, l, g * gw), BF16),
        scratch_shapes=[pltpu.VMEM((2, SSD_STATE, gw), F32),
                        pltpu.VMEM((l, gw), F32)],
        compiler_params=_cparams("parallel", "parallel"),
        name="ssd_scan",
    )(xbc, xbc, xbc, proj, cum3, dt3, cumr, dtr, e_seg, e_ch, dskip, norm_w)


GLA_TILE = 256
GLA_TILES_PER_STEP = 2


def _split_dot(m, x):
    x_hi = x.astype(BF16)
    x_lo = (x - x_hi.astype(F32)).astype(BF16)
    return (jnp.dot(m, x_hi, preferred_element_type=F32) + jnp.dot(m, x_lo, preferred_element_type=F32))


def _split_dot3(a, b):
    a_hi = a.astype(BF16)
    a_lo = (a - a_hi.astype(F32)).astype(BF16)
    b_hi = b.astype(BF16)
    b_lo = (b - b_hi.astype(F32)).astype(BF16)
    return (jnp.dot(a_hi, b_hi, preferred_element_type=F32) + jnp.dot(a_lo, b_hi, preferred_element_type=F32)
            + jnp.dot(a_hi, b_lo, preferred_element_type=F32))


def _gla_kernel(q_ref, k_ref, v_ref, r_ref, lr_ref, w2f_ref, b2f_ref, w2b_ref, b2b_ref, nw_ref,
                o_ref, st_ref, y_ref, qd_ref, kt_ref, eb_ref, *, l, lc):
    c = GLA_CHUNK
    t = GLA_TILE
    n_lat, n_ctx = l // c, lc // c
    n_all = n_lat + n_ctx
    hk = q_ref.shape[2]
    scale = hk ** -0.5
    row = lax.broadcasted_iota(jnp.int32, (t, t), 0)
    colm = lax.broadcasted_iota(jnp.int32, (t, t), 1)
    same_chunk = (row // c) == (colm // c)
    masks = (same_chunk & (row >= colm), same_chunk & (row <= colm))

    def gates(off, emit, n_tiles):
        rows = n_tiles * t
        tok = pl.ds(off, rows)
        tiles = [slice(i * t, (i + 1) * t) for i in range(n_tiles)]
        per_tile = lambda f: jnp.concatenate([f(ts) for ts in tiles], axis=0)
        kf = k_ref[0, tok, :].astype(F32)
        lr_all = lr_ref[0, tok, :]
        for d in range(2):
            w2 = w2f_ref if d == 0 else w2b_ref
            b2 = b2f_ref if d == 0 else b2b_ref
            lr = lr_all[:, d * GLA_RANK:(d + 1) * GLA_RANK]
            logit = _split_dot3(lr, w2[...]) + b2[...]
            g = -_softplus(-logit) * (1.0 / GLA_NORMALIZER)
            tri = masks[d].astype(BF16)
            bcum = per_tile(lambda ts: _split_dot(tri, g[ts]))
            edge = c - 1 if d == 0 else 0
            btot = jnp.concatenate(
                [jnp.broadcast_to(bcum[j * c + edge:j * c + edge + 1, :], (c, hk)) for j in range(rows // c)],
                axis=0)
            kt_ref[d, tok, :] = (kf * jnp.exp(btot - bcum)).astype(BF16)
            ebt = jnp.exp(btot)
            for j in range(rows // c):
                eb_ref[d, pl.ds(off // c + j, 1)] = ebt[j * c:j * c + 8, :][None]
            if emit:
                qd = (q_ref[0, tok, :].astype(F32) * scale * jnp.exp(bcum)).astype(BF16)
                kd = (kf * jnp.exp(-bcum)).astype(BF16)
                qd_ref[d, tok, :] = qd

                def intra(ts):
                    a = lax.dot_general(qd[ts], kd[ts], NT_DIMS, preferred_element_type=F32)
                    a = jnp.where(masks[d], a, 0.0).astype(BF16)
                    return jnp.dot(a, v_ref[0, pl.ds(off + ts.start, t), :], preferred_element_type=F32)

                o = per_tile(intra)
                if d == 0:
                    y_ref[tok, :] = o
                else:
                    y_ref[tok, :] = y_ref[tok, :] + o

    def lat_gates(i, carry):
        gates(pl.multiple_of(i * (GLA_TILES_PER_STEP * t), GLA_TILES_PER_STEP * t), True, GLA_TILES_PER_STEP)
        return carry

    lax.fori_loop(0, l // (GLA_TILES_PER_STEP * t), lat_gates, 0)
    for i in range(lc // t):
        gates(l + i * t, False, 1)

    def step(ci, d, emit):
        tok = pl.ds(pl.multiple_of(ci * c, c), c)
        st = st_ref[d]
        if emit:
            o = lax.dot_general(qd_ref[d, tok, :], st.astype(BF16), NT_DIMS, preferred_element_type=F32)
            y_ref[tok, :] = y_ref[tok, :] + o
        st_ref[d] = st * eb_ref[d, ci][0:1, :] + lax.dot_general(
            v_ref[0, tok, :], kt_ref[d, tok, :], TN_DIMS, preferred_element_type=F32)

    def ctx_step(i, carry):
        step(n_lat + i, 0, False)
        step(n_all - 1 - i, 1, False)
        return carry

    def lat_step(i, carry):
        step(i, 0, True)
        step(n_lat - 1 - i, 1, True)
        return carry

    st_ref[...] = jnp.zeros_like(st_ref)
    lax.fori_loop(0, n_ctx, ctx_step, 0)
    lax.fori_loop(0, n_lat, lat_step, 0)

    rows = 256
    for t0 in range(0, l, rows):
        o = y_ref[t0:t0 + rows, :]
        ms = jnp.mean(o * o, axis=-1, keepdims=True)
        o = o * lax.rsqrt(ms + EPS) * nw_ref[...]
        o_ref[0, t0:t0 + rows, :] = (o * _silu(r_ref[0, t0:t0 + rows, :].astype(F32))).astype(o_ref.dtype)


def _gla(proj, small, w2f, b2f, w2b, b2b, norm_w, l, lc, blocks):
    b, s, _ = proj.shape
    h = GLA_HEADS
    hk = w2f.shape[1] // h
    hv = norm_w.shape[0]
    q0, k0, v0, r0 = blocks
    kern = functools.partial(_gla_kernel, l=l, lc=lc)
    return pl.pallas_call(
        kern,
        grid=(b, h),
        in_specs=[pl.BlockSpec((1, s, hk), lambda i, j: (i, 0, q0 // hk + j)),
                  pl.BlockSpec((1, s, hk), lambda i, j: (i, 0, k0 // hk + j)),
                  pl.BlockSpec((1, s, hv), lambda i, j: (i, 0, v0 // hv + j)),
                  pl.BlockSpec((1, l, hv), lambda i, j: (i, 0, r0 // hv + j)),
                  pl.BlockSpec((1, s, LANES), lambda i, j: (i, 0, 1)),
                  pl.BlockSpec((GLA_RANK, hk), lambda i, j: (0, j)),
                  pl.BlockSpec((1, hk), lambda i, j: (0, j)),
                  pl.BlockSpec((GLA_RANK, hk), lambda i, j: (0, j)),
                  pl.BlockSpec((1, hk), lambda i, j: (0, j)),
                  pl.BlockSpec((1, hv), lambda i, j: (0, 0))],
        out_specs=pl.BlockSpec((1, l, hv), lambda i, j: (i, 0, j)),
        out_shape=jax.ShapeDtypeStruct((b, l, h * hv), BF16),
        scratch_shapes=[pltpu.VMEM((2, hv, hk), F32),
                        pltpu.VMEM((l, hv), F32),
                        pltpu.VMEM((2, l, hk), BF16),
                        pltpu.VMEM((2, s, hk), BF16),
                        pltpu.VMEM((2, s // GLA_CHUNK, 8, hk), F32)],
        compiler_params=_cparams("parallel", "parallel"),
        name="gla_scan",
    )(proj, proj, proj, proj, small, w2f, b2f.reshape(1, -1), w2b, b2b.reshape(1, -1),
      norm_w.reshape(1, hv))


def _merge_kernel(a_ref, y_ref, wa_ref, wb_ref, gla_ref, glb_ref, ba_ref, bb_ref, o_ref):
    ya = jnp.dot(a_ref[0], wa_ref[...], preferred_element_type=F32)
    yb = jnp.dot(y_ref[0], wb_ref[...], preferred_element_type=F32)
    ga = jax.nn.sigmoid(gla_ref[0].astype(F32) + ba_ref[...])
    gb = jax.nn.sigmoid(glb_ref[0].astype(F32) + bb_ref[...])
    o_ref[0] = (ga * ya + gb * yb).astype(o_ref.dtype)


def _merge(a_n, y_n, w_a, w_b, proj, gl0, b_gate):
    b, l, ka = a_n.shape
    kb = y_n.shape[2]
    d = w_a.shape[1]
    bm = _pick(l, (1024, 512, 256))
    bn = 512
    nb = d // bn
    bg = b_gate.reshape(1, 2 * d)
    return pl.pallas_call(
        _merge_kernel,
        grid=(b, l // bm, nb),
        in_specs=[pl.BlockSpec((1, bm, ka), lambda i, t, j: (i, t, 0)),
                  pl.BlockSpec((1, bm, kb), lambda i, t, j: (i, t, 0)),
                  pl.BlockSpec((ka, bn), lambda i, t, j: (0, j)),
                  pl.BlockSpec((kb, bn), lambda i, t, j: (0, j)),
                  pl.BlockSpec((1, bm, bn), lambda i, t, j: (i, t, gl0 // bn + j)),
                  pl.BlockSpec((1, bm, bn), lambda i, t, j: (i, t, gl0 // bn + nb + j)),
                  pl.BlockSpec((1, bn), lambda i, t, j: (0, j)),
                  pl.BlockSpec((1, bn), lambda i, t, j: (0, nb + j))],
        out_specs=pl.BlockSpec((1, bm, bn), lambda i, t, j: (i, t, j)),
        out_shape=jax.ShapeDtypeStruct((b, l, d), BF16),
        compiler_params=_cparams("parallel", "parallel", "parallel"),
        name="branch_merge",
    )(a_n, y_n, w_a, w_b, proj, proj, bg, bg)


def _resid_kernel(m_ref, w_ref, x_ref, g_ref, o_ref):
    y = jnp.dot(m_ref[0], w_ref[...], preferred_element_type=F32)
    o_ref[0] = x_ref[0] + g_ref[0] * y


def _resid(m, w_o, x, gate):
    b, l, d = x.shape
    bm = _pick(l, (512, 256))
    bn = d
    return pl.pallas_call(
        _resid_kernel,
        grid=(b, l // bm, d // bn),
        in_specs=[pl.BlockSpec((1, bm, d), lambda i, t, j: (i, t, 0)),
                  pl.BlockSpec((d, bn), lambda i, t, j: (0, j)),
                  pl.BlockSpec((1, bm, bn), lambda i, t, j: (i, t, j)),
                  pl.BlockSpec((1, 1, bn), lambda i, t, j: (i, 0, j))],
        out_specs=pl.BlockSpec((1, bm, bn), lambda i, t, j: (i, t, j)),
        out_shape=jax.ShapeDtypeStruct((b, l, d), F32),
        compiler_params=_cparams("parallel", "parallel", "parallel"),
        name="attn_resid",
    )(m, w_o, x, gate)


def _peer_q_kernel(x_ref, nw_ref, sh_ref, sc_ref, wq_ref, h_ref, q_ref, lhs_ref):
    @pl.when(pl.program_id(2) == 0)
    def _():
        xf = x_ref[0]
        ms = jnp.mean(xf * xf, axis=-1, keepdims=True)
        y = xf * lax.rsqrt(ms + EPS) * nw_ref[...]
        hb = (y * (1.0 + sc_ref[0]) + sh_ref[0]).astype(BF16)
        lhs_ref[...] = hb
        h_ref[0] = hb

    q_ref[0] = jnp.dot(lhs_ref[...], wq_ref[...], preferred_element_type=F32)


def _peer_q(x1, norm_w, shift, scale, wq):
    b, l, d = x1.shape
    n = wq.shape[1]
    bm = _pick(l, (512, 256))
    bn = n
    return pl.pallas_call(
        _peer_q_kernel,
        grid=(b, l // bm, n // bn),
        in_specs=[pl.BlockSpec((1, bm, d), lambda i, t, j: (i, t, 0)),
                  pl.BlockSpec((1, d), lambda i, t, j: (0, 0)),
                  pl.BlockSpec((1, 1, d), lambda i, t, j: (i, 0, 0)),
                  pl.BlockSpec((1, 1, d), lambda i, t, j: (i, 0, 0)),
                  pl.BlockSpec((d, bn), lambda i, t, j: (0, j))],
        out_specs=[pl.BlockSpec((1, bm, d), lambda i, t, j: (i, t, 0)),
                   pl.BlockSpec((1, bm, bn), lambda i, t, j: (i, t, j))],
        out_shape=[jax.ShapeDtypeStruct((b, l, d), BF16),
                   jax.ShapeDtypeStruct((b, l, n), F32)],
        scratch_shapes=[pltpu.VMEM((bm, d), BF16)],
        compiler_params=_cparams("parallel", "parallel", "arbitrary"),
        name="peer_query",
    )(x1, norm_w.reshape(1, d), shift, scale, wq)


def _top_values(x, k, rows):
    sub = lax.broadcasted_iota(jnp.int32, (rows, x.shape[1]), 0)
    out = jnp.full((rows, x.shape[1]), -jnp.inf, F32)
    for i in range(k):
        m = jnp.max(x, axis=0, keepdims=True)
        out = jnp.where(sub == i, m, out)
        x = jnp.where(x == m, -jnp.inf, x)
    return out


def _peer_score_kernel(q_ref, keys_ref, c1_ref, e1_ref, s2_ref, e2_ref):
    k = PEER_TOPK
    dk = keys_ref.shape[3]
    q = q_ref[...]
    s1 = lax.dot_general(keys_ref[0, 0], q[:, 0:dk], NT_DIMS, precision=HIGHEST,
                         preferred_element_type=F32)
    s2 = lax.dot_general(keys_ref[0, 1], q[:, dk:2 * dk], NT_DIMS, precision=HIGHEST,
                         preferred_element_type=F32)
    n = k + 1
    pad = -(-n // 8) * 8
    sv1 = _top_values(s1, n, pad)
    sv2 = _top_values(s2, n, pad)
    assert n // 2 <= 8 and n // 9 == 1
    cand = jnp.concatenate(
        [sv1[0:1, :] + sv2]
        + [sv1[a:a + 1, :] + sv2[0:8, :] for a in range(1, 8)]
        + [sv1[8:pad, :] + sv2[0:1, :]], axis=0)
    cv = _top_values(cand, n, pad)
    z = jnp.sum(jnp.exp(cv[0:k, :] - cv[0:1, :]), axis=0, keepdims=True)
    thr = 0.5 * (cv[k - 1:k, :] + cv[k:k + 1, :])
    c1_ref[0] = thr - s1
    s2_ref[0] = s2
    e1_ref[0] = jnp.exp(s1 - sv1[0:1, :]) / z
    e2_ref[0] = jnp.exp(s2 - sv2[0:1, :])


def _peer_scores(q, keys):
    t, n = q.shape
    h, _, nk, dk = keys.shape
    tt = _pick(t, (512, 256))
    big = jax.ShapeDtypeStruct((h, nk, t), F32)
    big_spec = pl.BlockSpec((1, nk, tt), lambda i, j: (j, 0, i))
    return pl.pallas_call(
        _peer_score_kernel,
        grid=(t // tt, h),
        in_specs=[pl.BlockSpec((tt, 2 * dk), lambda i, j: (i, j)),
                  pl.BlockSpec((1, 2, nk, dk), lambda i, j: (j, 0, 0, 0))],
        out_specs=[big_spec, big_spec, big_spec, big_spec],
        out_shape=[big, big, big, big],
        compiler_params=_cparams("parallel", "parallel"),
        name="peer_scores",
    )(q, keys)


PEER_EXPERT_BLOCK = 1024
GATE_ROWS = 8


def _peer_expert_kernel(h_ref, u_ref, vt_ref, c1_ref, e1_ref, s2_ref, e2_ref,
                        o_ref, ht_ref, w_ref, c8_ref, e8_ref, *, n_sub):
    j = pl.program_id(1)
    nh, nk, bm = s2_ref.shape
    sub = 8

    @pl.when(j == 0)
    def _():
        o_ref[...] = jnp.zeros_like(o_ref)
        ht_ref[...] = h_ref[...].astype(F32).T.astype(BF16)

    i1_0 = pl.multiple_of(j * n_sub, n_sub)
    for hd in range(nh):
        cgrp = c1_ref[hd, pl.ds(i1_0, n_sub), :]
        egrp = e1_ref[hd, pl.ds(i1_0, n_sub), :]
        for sb in range(n_sub):
            r0 = (hd * n_sub + sb) * sub
            c8_ref[r0:r0 + sub, :] = jnp.broadcast_to(cgrp[sb:sb + 1, :], (sub, bm))
            e8_ref[r0:r0 + sub, :] = jnp.broadcast_to(egrp[sb:sb + 1, :], (sub, bm))

    act = jnp.dot(u_ref[...], ht_ref[...], preferred_element_type=F32)

    n_part = GATE_ROWS // sub

    for lc in range(bm // LANES):
        ls = slice(lc * LANES, (lc + 1) * LANES)
        for rc in range(nk // GATE_ROWS):
            base = rc * GATE_ROWS
            accs = [[jnp.zeros((sub, LANES), F32) for _ in range(n_part)] for _ in range(n_sub)]
            for hd in range(nh):
                s2v = [s2_ref[hd, base + k * sub:base + (k + 1) * sub, ls] for k in range(n_part)]
                e2v = [e2_ref[hd, base + k * sub:base + (k + 1) * sub, ls] for k in range(n_part)]
                for sb in range(n_sub):
                    r0 = (hd * n_sub + sb) * sub
                    cut = c8_ref[r0:r0 + sub, ls]
                    e1v = e8_ref[r0:r0 + sub, ls]
                    for k in range(n_part):
                        accs[sb][k] = accs[sb][k] + jnp.where(s2v[k] >= cut, e2v[k], 0.0) * e1v
            for sb in range(n_sub):
                for k in range(n_part):
                    r0 = sb * nk + base + k * sub
                    w_ref[r0:r0 + sub, ls] = accs[sb][k]

    gelu = 0.5 * act * (1.0 + lax.erf(act * (2.0 ** -0.5)))
    pmat = (gelu * w_ref[...]).astype(BF16)
    o_ref[...] += jnp.dot(vt_ref[...], pmat, preferred_element_type=F32)


def _peer_experts(h2, u, vt, c1, e1, s2, e2):
    t, d = h2.shape
    e = u.shape[0]
    nh, nk, _ = s2.shape
    bm = _pick(t, (512, 256))
    eb = PEER_EXPERT_BLOCK
    n_sub = eb // nk
    kern = functools.partial(_peer_expert_kernel, n_sub=n_sub)
    big_spec = pl.BlockSpec((nh, nk, bm), lambda i, j: (0, 0, i))
    return pl.pallas_call(
        kern,
        grid=(t // bm, e // eb),
        in_specs=[pl.BlockSpec((bm, d), lambda i, j: (i, 0)),
                  pl.BlockSpec((eb, d), lambda i, j: (j, 0)),
                  pl.BlockSpec((d, eb), lambda i, j: (0, j)),
                  big_spec, big_spec, big_spec, big_spec],
        out_specs=pl.BlockSpec((d, bm), lambda i, j: (0, i)),
        out_shape=jax.ShapeDtypeStruct((d, t), F32),
        scratch_shapes=[pltpu.VMEM((d, bm), BF16),
                        pltpu.VMEM((eb, bm), F32),
                        pltpu.VMEM((nh * n_sub * 8, bm), F32),
                        pltpu.VMEM((nh * n_sub * 8, bm), F32)],
        compiler_params=_cparams("parallel", "arbitrary"),
        name="peer_experts",
    )(h2, u, vt, c1, e1, s2, e2)


def _final_kernel(x_ref, yt_ref, g_ref, w_ref, o_ref):
    xf = x_ref[0] + g_ref[0] * yt_ref[...].T
    ms = jnp.mean(xf * xf, axis=-1, keepdims=True)
    o_ref[0] = xf * lax.rsqrt(ms + EPS) * w_ref[...]


def _final(x1, yt, gate, w):
    b, l, d = x1.shape
    bm = _pick(l, (512, 256))
    nt = l // bm
    return pl.pallas_call(
        _final_kernel,
        grid=(b, l // bm),
        in_specs=[pl.BlockSpec((1, bm, d), lambda i, t: (i, t, 0)),
                  pl.BlockSpec((d, bm), lambda i, t: (0, i * nt + t)),
                  pl.BlockSpec((1, 1, d), lambda i, t: (i, 0, 0)),
                  pl.BlockSpec((1, d), lambda i, t: (0, 0))],
        out_specs=pl.BlockSpec((1, bm, d), lambda i, t: (i, t, 0)),
        out_shape=jax.ShapeDtypeStruct((b, l, d), F32),
        compiler_params=_cparams("parallel", "parallel"),
        name="final_norm",
    )(x1, yt, gate, w.reshape(1, d))


def _layer(x, ctx, mod_x, mod_c, p):
    b, l, d = x.shape
    lc = ctx.shape[1]
    s = l + lc
    dk = p['w_lr2_f'].shape[1]
    dv = p['w_gla_out'].shape[0]
    di = p['w_ssd_out'].shape[0]
    bc = SSD_GROUPS * SSD_STATE
    n_ssd_heads = p['a_log_f'].shape[0]

    sizes = (dk, dk, dv, dv, GLA_RANK, GLA_RANK, di, di + 2 * bc, n_ssd_heads, n_ssd_heads, 2 * d)
    offs = [0]
    for sz in sizes:
        offs.append(offs[-1] + sz)
    w_in = p['w_in']
    seg = lambda i: w_in[:, offs[i]:offs[i + 1]]
    w_main = jnp.concatenate([seg(0), seg(1), seg(2), seg(3), seg(6), seg(7), seg(10)], axis=1).astype(BF16)
    n_small = 2 * LANES
    w_small = jnp.concatenate(
        [seg(8), seg(9), seg(4), seg(5),
         jnp.zeros((d, n_small - 2 * n_ssd_heads - 2 * GLA_RANK), w_in.dtype)], axis=1).astype(BF16)
    q0, k0, v0, r0 = 0, dk, 2 * dk, 2 * dk + dv
    z0 = r0 + dv
    xbc0 = z0 + di
    gl0 = xbc0 + di + 2 * bc

    shift1 = jnp.stack([mod_x[0], jnp.broadcast_to(mod_c[0], (b, d))], axis=1).reshape(b, 2, 1, d)
    scale1 = jnp.stack([mod_x[1], jnp.broadcast_to(mod_c[1], (b, d))], axis=1).reshape(b, 2, 1, d)
    h = _norm_mod(x, ctx, p['norm1_w'], shift1, scale1)

    h2d = h.reshape(b * s, d)
    proj = _matmul(h2d, w_main, BF16, "in_proj").reshape(b, s, -1)
    small = _matmul(h2d, w_small, F32, "in_proj_gates").reshape(b, s, n_small)

    xbc = _conv(proj, xbc0 // 512, di + 2 * bc, p['conv_w'], p['conv_b'], l, lc)
    bias = jnp.concatenate([p['dt_bias_f'], p['dt_bias_b']]).reshape(1, LANES).astype(F32)
    a_neg = -jnp.exp(jnp.concatenate([p['a_log_f'], p['a_log_b']]).astype(F32)).reshape(1, LANES)
    dtr, cumr, dt3, cum3 = _ssd_prep(small, bias, a_neg)
    g, hpg, c = SSD_GROUPS, SSD_HPG, SSD_CHUNK
    per_dir = lambda t: t.reshape(b, 2, g, s // c, hpg, c)
    dskip = jnp.repeat(p['d_skip'].astype(F32), SSD_HEADDIM).reshape(g, 1, hpg * SSD_HEADDIM)
    ssd_nw = p['ssd_norm_w'].astype(F32).reshape(g, 1, hpg * SSD_HEADDIM)
    y_n = _ssd(xbc, proj, z0 // 512, _split3_cols(cum3, b, s), _split3_cols(dt3, b, s),
               per_dir(cumr), per_dir(dtr), dskip, ssd_nw, l, lc)

    a_n = _gla(proj, small, p['w_lr2_f'], p['b_lr_f'], p['w_lr2_b'], p['b_lr_b'], p['gla_norm_w'],
               l, lc, (q0, k0, v0, r0))

    m = _merge(a_n, y_n, p['w_gla_out'].astype(BF16), p['w_ssd_out'].astype(BF16), proj, gl0,
               p['b_gate'])
    return _resid(m, p['w_o'].astype(BF16), x, mod_x[2].reshape(b, 1, d))


def _peer(x1, mod_x, norm_w, wq, keys, u, v):
    b, l, d = x1.shape
    h2, q = _peer_q(x1, norm_w, mod_x[3].reshape(b, 1, d), mod_x[4].reshape(b, 1, d), wq.astype(BF16))
    c1, e1, s2, e2 = _peer_scores(q.reshape(b * l, -1), keys)
    return _peer_experts(h2.reshape(b * l, d), u.astype(BF16), v.astype(BF16).T, c1, e1, s2, e2)


def kernel(x, c, ctx, c_ctx, w_ada, b_ada, norm1_w, w_in, b_gate, w_lr2_f, b_lr_f, w_lr2_b, b_lr_b,
           gla_norm_w, w_gla_out, conv_w, conv_b, a_log_f, a_log_b, dt_bias_f, dt_bias_b, d_skip,
           ssd_norm_w, w_ssd_out, w_o, norm2_w, peer_wq, peer_keys, peer_u, peer_v, final_norm_w):
    b, l, d = x.shape
    depth = w_in.shape[0]
    assert depth == 1, "context-stream update for deeper stacks is not implemented"
    layer = 0
    rows = -(-(b + 1) // 8) * 8
    c_all = jnp.concatenate([c, c_ctx[None, :], jnp.zeros((rows - b - 1, d), c.dtype)], axis=0)
    mod = _ada(c_all, w_ada[layer], b_ada[layer])
    mod_x = [mod[:b, i * d:(i + 1) * d] for i in range(N_MOD)]
    mod_c = [mod[b, i * d:(i + 1) * d] for i in range(N_MOD)]
    p = {
        'norm1_w': norm1_w[layer], 'w_in': w_in[layer], 'b_gate': b_gate[layer],
        'w_lr2_f': w_lr2_f[layer], 'b_lr_f': b_lr_f[layer],
        'w_lr2_b': w_lr2_b[layer], 'b_lr_b': b_lr_b[layer],
        'gla_norm_w': gla_norm_w[layer], 'w_gla_out': w_gla_out[layer],
        'conv_w': conv_w[layer], 'conv_b': conv_b[layer],
        'a_log_f': a_log_f[layer], 'a_log_b': a_log_b[layer],
        'dt_bias_f': dt_bias_f[layer], 'dt_bias_b': dt_bias_b[layer],
        'd_skip': d_skip[layer], 'ssd_norm_w': ssd_norm_w[layer],
        'w_ssd_out': w_ssd_out[layer], 'w_o': w_o[layer],
    }
    x1 = _layer(x, ctx, mod_x, mod_c, p)
    y = _peer(x1, mod_x, norm2_w[layer], peer_wq[layer], peer_keys[layer], peer_u[layer], peer_v[layer])
    return _final(x1, y, mod_x[5].reshape(b, 1, d), final_norm_w)
```

```python
import functools

import jax
import jax.numpy as jnp
from jax import lax
from jax.experimental import pallas as pl
from jax.experimental.pallas import tpu as pltpu

F32 = jnp.float32
BF16 = jnp.bfloat16
EPS = 1e-6
HIGHEST = lax.Precision.HIGHEST

N_MOD = 6
GRID_W = 64
GLA_HEADS = 4
GLA_RANK = 16
GLA_NORMALIZER = 16.0
GLA_CHUNK = 64
SSD_HEADDIM = 64
SSD_GROUPS = 8
SSD_HPG = 8
SSD_STATE = 128
SSD_CHUNK = 128
PEER_HEADS = 8
PEER_NKEYS = 128
PEER_TOPK = 16

VMEM_LIMIT_BYTES = 56 * 1024 * 1024
LANES = 128

NT_DIMS = (((1,), (1,)), ((), ()))
TN_DIMS = (((0,), (0,)), ((), ()))


def _cparams(*sem, flags=None):
    return pltpu.CompilerParams(dimension_semantics=sem, vmem_limit_bytes=VMEM_LIMIT_BYTES, flags=flags)


def _pick(n, options):
    for o in options:
        if n % o == 0:
            return o
    raise ValueError(f"no tile in {options} divides {n}")


def _softplus(x):
    return jnp.maximum(x, 0.0) + jnp.log1p(jnp.exp(-jnp.abs(x)))


def _silu(x):
    return x * jax.nn.sigmoid(x)


def _ada_kernel(c_ref, w_ref, b_ref, o_ref):
    a = _silu(c_ref[...])
    o_ref[...] = jnp.dot(a, w_ref[...], precision=HIGHEST, preferred_element_type=F32) + b_ref[...]


def _ada(c_all, w_ada, b_ada):
    m, d = c_all.shape
    n = w_ada.shape[1]
    bn = _pick(n, (1024, 512, 256, 128))
    return pl.pallas_call(
        _ada_kernel,
        grid=(n // bn,),
        in_specs=[pl.BlockSpec((m, d), lambda j: (0, 0)),
                  pl.BlockSpec((d, bn), lambda j: (0, j)),
                  pl.BlockSpec((1, bn), lambda j: (0, j))],
        out_specs=pl.BlockSpec((m, bn), lambda j: (0, j)),
        out_shape=jax.ShapeDtypeStruct((m, n), F32),
        compiler_params=_cparams("parallel"),
        name="ada_mod",
    )(c_all, w_ada, b_ada.reshape(1, n))


def _norm_mod_kernel(x_ref, ctx_ref, w_ref, sh_ref, sc_ref, o_ref, *, n_lat_tiles):
    j = pl.program_id(1)

    def emit(src):
        xf = src[0]
        ms = jnp.mean(xf * xf, axis=-1, keepdims=True)
        y = xf * lax.rsqrt(ms + EPS) * w_ref[...]
        o_ref[0] = (y * (1.0 + sc_ref[0, 0]) + sh_ref[0, 0]).astype(o_ref.dtype)

    @pl.when(j < n_lat_tiles)
    def _():
        emit(x_ref)

    @pl.when(j >= n_lat_tiles)
    def _():
        emit(ctx_ref)


def _norm_mod(x, ctx, w, shift, scale):
    b, l, d = x.shape
    lc = ctx.shape[1]
    tn = _pick(lc, (256, 128))
    nl, nc = l // tn, lc // tn
    kern = functools.partial(_norm_mod_kernel, n_lat_tiles=nl)
    return pl.pallas_call(
        kern,
        grid=(b, nl + nc),
        in_specs=[pl.BlockSpec((1, tn, d), lambda i, j: (i, jnp.minimum(j, nl - 1), 0)),
                  pl.BlockSpec((1, tn, d), lambda i, j: (i, jnp.maximum(j - nl, 0), 0)),
                  pl.BlockSpec((1, d), lambda i, j: (0, 0)),
                  pl.BlockSpec((1, 1, 1, d), lambda i, j: (i, j // nl, 0, 0)),
                  pl.BlockSpec((1, 1, 1, d), lambda i, j: (i, j // nl, 0, 0))],
        out_specs=pl.BlockSpec((1, tn, d), lambda i, j: (i, j, 0)),
        out_shape=jax.ShapeDtypeStruct((b, l + lc, d), BF16),
        compiler_params=_cparams("parallel", "parallel"),
        name="norm_mod",
    )(x, ctx, w.reshape(1, d), shift, scale)


def _matmul_kernel(a_ref, b_ref, o_ref):
    o_ref[...] = jnp.dot(a_ref[...], b_ref[...], preferred_element_type=F32).astype(o_ref.dtype)


def _matmul(a, b, out_dtype, name):
    m, k = a.shape
    n = b.shape[1]
    bm = _pick(m, (2048, 1024, 512, 256))
    bn = _pick(n, (1024, 512, 256))
    return pl.pallas_call(
        _matmul_kernel,
        grid=(m // bm, n // bn),
        in_specs=[pl.BlockSpec((bm, k), lambda i, j: (i, 0)),
                  pl.BlockSpec((k, bn), lambda i, j: (0, j))],
        out_specs=pl.BlockSpec((bm, bn), lambda i, j: (i, j)),
        out_shape=jax.ShapeDtypeStruct((m, n), out_dtype),
        compiler_params=_cparams("parallel", "parallel"),
        name=name,
    )(a, b)


CONV_TOK = 256
CTX_HALO = 8
LAT_HALO = GRID_W + 8


def _conv_kernel(x_ref, w_ref, b_ref, o_ref, lat_ref, ctx_ref, *, l, lc):
    tc = x_ref.shape[2]
    lat_ref[0:LAT_HALO, :] = jnp.zeros((LAT_HALO, tc), F32)
    lat_ref[LAT_HALO + l:LAT_HALO + l + LAT_HALO, :] = jnp.zeros((LAT_HALO, tc), F32)
    ctx_ref[0:CTX_HALO, :] = jnp.zeros((CTX_HALO, tc), F32)
    ctx_ref[CTX_HALO + lc:CTX_HALO + lc + CTX_HALO, :] = jnp.zeros((CTX_HALO, tc), F32)
    lat_ref[LAT_HALO:LAT_HALO + l, :] = x_ref[0, 0:l, :].astype(F32)
    ctx_ref[CTX_HALO:CTX_HALO + lc, :] = x_ref[0, l:l + lc, :].astype(F32)

    bias = b_ref[...]
    col = lax.broadcasted_iota(jnp.int32, (CONV_TOK, tc), 0) % GRID_W
    not_first = col != 0
    not_last = col != GRID_W - 1

    for t0 in range(0, l, CONV_TOK):
        acc = jnp.zeros((CONV_TOK, tc), F32)
        for kw in range(3):
            part = jnp.zeros((CONV_TOK, tc), F32)
            for kh in range(3):
                start = LAT_HALO + t0 + GRID_W * (kh - 1) + (kw - 1)
                part = part + w_ref[kh * 3 + kw:kh * 3 + kw + 1, :] * lat_ref[start:start + CONV_TOK, :]
            if kw == 0:
                part = jnp.where(not_first, part, 0.0)
            elif kw == 2:
                part = jnp.where(not_last, part, 0.0)
            acc = acc + part
        o_ref[0, t0:t0 + CONV_TOK, :] = _silu(acc + bias).astype(o_ref.dtype)

    ctx_tok = min(CONV_TOK, lc)
    for t0 in range(0, lc, ctx_tok):
        acc = jnp.zeros((ctx_tok, tc), F32)
        for kw in range(3):
            start = CTX_HALO + t0 + (kw - 1)
            acc = acc + w_ref[3 + kw:4 + kw, :] * ctx_ref[start:start + ctx_tok, :]
        o_ref[0, l + t0:l + t0 + ctx_tok, :] = _silu(acc + bias).astype(o_ref.dtype)


def _conv(proj, col_block0, n_ch, conv_w, conv_b, l, lc):
    b, s, _ = proj.shape
    tc = 512
    kern = functools.partial(_conv_kernel, l=l, lc=lc)
    return pl.pallas_call(
        kern,
        grid=(b, n_ch // tc),
        in_specs=[pl.BlockSpec((1, s, tc), lambda i, j: (i, 0, col_block0 + j)),
                  pl.BlockSpec((9, tc), lambda i, j: (0, j)),
                  pl.BlockSpec((1, tc), lambda i, j: (0, j))],
        out_specs=pl.BlockSpec((1, s, tc), lambda i, j: (i, 0, j)),
        out_shape=jax.ShapeDtypeStruct((b, s, n_ch), BF16),
        scratch_shapes=[pltpu.VMEM((l + 2 * LAT_HALO, tc), F32),
                        pltpu.VMEM((lc + 2 * CTX_HALO, tc), F32)],
        compiler_params=_cparams("parallel", "parallel"),
        name="dwconv_silu",
    )(proj, conv_w.reshape(9, n_ch), conv_b.reshape(1, n_ch))


def _bf16_terms(x):
    hi = x.astype(BF16)
    r1 = x - hi.astype(F32)
    mid = r1.astype(BF16)
    lo = (r1 - mid.astype(F32)).astype(BF16)
    return hi, mid, lo


def _ssd_prep_kernel(x_ref, bias_ref, a_ref, dtr_ref, cumr_ref, dt3_ref, cum3_ref):
    c = SSD_CHUNK
    s = x_ref.shape[1]
    n_dg = LANES // SSD_HPG
    row = lax.broadcasted_iota(jnp.int32, (c, c), 0)
    colm = lax.broadcasted_iota(jnp.int32, (c, c), 1)
    tri_f = (row >= colm).astype(F32)
    tri_b = (row <= colm).astype(F32)
    is_fwd = lax.broadcasted_iota(jnp.int32, (c, LANES), 1) < (LANES // 2)
    for t0 in range(0, s, c):
        dt = _softplus(x_ref[0, t0:t0 + c, :] + bias_ref[...])
        dta = dt * a_ref[...]
        cf = jnp.dot(tri_f, dta, precision=HIGHEST, preferred_element_type=F32)
        cb = jnp.dot(tri_b, dta, precision=HIGHEST, preferred_element_type=F32)
        cum = jnp.where(is_fwd, cf, cb)
        dt_t, cum_t = dt.T, cum.T
        for dg in range(n_dg):
            dtr_ref[0, dg, t0 // c] = dt_t[dg * SSD_HPG:(dg + 1) * SSD_HPG, :]
            cumr_ref[0, dg, t0 // c] = cum_t[dg * SSD_HPG:(dg + 1) * SSD_HPG, :]
        for term, (dt_t, cum_t) in enumerate(zip(_bf16_terms(dt), _bf16_terms(cum))):
            dt3_ref[0, term, t0:t0 + c, :] = dt_t
            cum3_ref[0, term, t0:t0 + c, :] = cum_t


def _ssd_prep(small, bias, a):
    b, s, _ = small.shape
    n_dg = LANES // SSD_HPG
    rows = jax.ShapeDtypeStruct((b, n_dg, s // SSD_CHUNK, SSD_HPG, SSD_CHUNK), F32)
    out3 = jax.ShapeDtypeStruct((b, 3, s, LANES), BF16)
    spec = pl.BlockSpec((1, s, LANES), lambda i: (i, 0, 0))
    rows_spec = pl.BlockSpec((1, n_dg, s // SSD_CHUNK, SSD_HPG, SSD_CHUNK), lambda i: (i, 0, 0, 0, 0))
    spec3 = pl.BlockSpec((1, 3, s, LANES), lambda i: (i, 0, 0, 0))
    return pl.pallas_call(
        _ssd_prep_kernel,
        grid=(b,),
        in_specs=[spec,
                  pl.BlockSpec((1, LANES), lambda i: (0, 0)),
                  pl.BlockSpec((1, LANES), lambda i: (0, 0))],
        out_specs=[rows_spec, rows_spec, spec3, spec3],
        out_shape=[rows, rows, out3, out3],
        compiler_params=_cparams("parallel"),
        name="ssd_prep",
    )(small, bias, a)


def _ssd_kernel(xs_ref, bm_ref, cm_ref, z_ref, cum3_ref, dt3_ref, cumr_ref, dtr_ref, e_seg_ref, e_ch_ref,
                dsk_ref, nw_ref, o_ref, s_ref, y_ref, *, l, lc):
    c = SSD_CHUNK
    p = SSD_HEADDIM
    nh = SSD_HPG
    n_lat, n_ctx = l // c, lc // c
    n_all = n_lat + n_ctx
    row = lax.broadcasted_iota(jnp.int32, (c, c), 0)
    colm = lax.broadcasted_iota(jnp.int32, (c, c), 1)
    masks = (row >= colm, row <= colm)

    def chunk(ci, d, emit):
        tok = pl.ds(pl.multiple_of(ci * c, c), c)
        last = c - 1 if d == 0 else 0
        x = xs_ref[0, tok, :]
        bmat = bm_ref[0, tok, :]
        cum3 = cum3_ref[0, d, tok, :]
        cum_ch = jnp.dot(cum3, e_ch_ref[...], preferred_element_type=F32)
        dt_ch = jnp.dot(dt3_ref[0, d, tok, :], e_ch_ref[...], preferred_element_type=F32)
        cum_end = cum_ch[last:last + 1, :]
        state = s_ref[d]
        if emit:
            cmat = cm_ref[0, tok, :]
            cumr = cumr_ref[0, d, 0, ci]
            dtr = dtr_ref[0, d, 0, ci]
            cum_seg = jnp.dot(cum3, e_seg_ref[...], preferred_element_type=F32)
            cb = lax.dot_general(cmat, bmat, NT_DIMS, preferred_element_type=F32)
            ys = []
            for h in range(nh):
                seg = cum_seg[:, h * c:(h + 1) * c] - cumr[h:h + 1, :]
                decay = jnp.exp(jnp.where(masks[d], seg, -jnp.inf))
                m = (cb * decay * dtr[h:h + 1, :]).astype(BF16)
                ys.append(jnp.dot(m, x[:, h * p:(h + 1) * p], preferred_element_type=F32))
            y = jnp.concatenate(ys, axis=1)
            y = y + jnp.dot(cmat, state.astype(BF16), preferred_element_type=F32) * jnp.exp(cum_ch)
            y_ref[tok, :] = y_ref[tok, :] + y
        xw = (x.astype(F32) * (jnp.exp(cum_end - cum_ch) * dt_ch)).astype(BF16)
        s_ref[d] = state * jnp.exp(cum_end) + lax.dot_general(bmat, xw, TN_DIMS,
                                                              preferred_element_type=F32)

    def ctx_step(i, carry):
        chunk(n_lat + i, 0, False)
        chunk(n_all - 1 - i, 1, False)
        return carry

    def lat_step(i, carry):
        chunk(i, 0, True)
        chunk(n_lat - 1 - i, 1, True)
        return carry

    rows = 256
    for t0 in range(0, l, rows):
        y_ref[t0:t0 + rows, :] = dsk_ref[0] * xs_ref[0, t0:t0 + rows, :].astype(F32)
    s_ref[...] = jnp.zeros_like(s_ref)
    lax.fori_loop(0, n_ctx, ctx_step, 0)
    lax.fori_loop(0, n_lat, lat_step, 0, unroll=2)

    rows = 256
    for t0 in range(0, l, rows):
        y = y_ref[t0:t0 + rows, :] * _silu(z_ref[0, t0:t0 + rows, :].astype(F32))
        ms = jnp.mean(y * y, axis=-1, keepdims=True)
        o_ref[0, t0:t0 + rows, :] = (y * lax.rsqrt(ms + EPS) * nw_ref[0]).astype(o_ref.dtype)


SPLIT_K = 32


def _split3_cols(t3, b, s):
    g, hpg = SSD_GROUPS, SSD_HPG
    parts = jnp.concatenate([t3, jnp.zeros_like(t3[:, :1])], axis=1)
    parts = parts.reshape(b, SPLIT_K // hpg, s, 2, g, hpg).transpose(0, 4, 3, 2, 1, 5)
    return parts.reshape(b * g, 2, s, SPLIT_K)


def _expansion(width):
    k = jnp.arange(SPLIT_K)[:, None]
    n = jnp.arange(SSD_HPG * width)[None, :]
    return ((k % SSD_HPG == n // width) & (k < 3 * SSD_HPG)).astype(BF16)


def _ssd(xbc, proj, z_block0, cum3, dt3, cumr, dtr, dskip, norm_w, l, lc):
    b, s, _ = xbc.shape
    g = SSD_GROUPS
    gw = SSD_HPG * SSD_HEADDIM
    nchunks = s // SSD_CHUNK
    xs_blocks = g * gw // SSD_STATE
    e_seg = _expansion(SSD_CHUNK)
    e_ch = _expansion(SSD_HEADDIM)
    kern = functools.partial(_ssd_kernel, l=l, lc=lc)
    col_spec = pl.BlockSpec((1, 2, s, SPLIT_K), lambda i, j: (i * g + j, 0, 0, 0))
    row_spec = pl.BlockSpec((1, 2, 1, nchunks, SSD_HPG, SSD_CHUNK), lambda i, j: (i, 0, j, 0, 0, 0))
    return pl.pallas_call(
        kern,
        grid=(b, g),
        in_specs=[pl.BlockSpec((1, s, gw), lambda i, j: (i, 0, j)),
                  pl.BlockSpec((1, s, SSD_STATE), lambda i, j: (i, 0, xs_blocks + j)),
                  pl.BlockSpec((1, s, SSD_STATE), lambda i, j: (i, 0, xs_blocks + g + j)),
                  pl.BlockSpec((1, l, gw), lambda i, j: (i, 0, z_block0 + j)),
                  col_spec, col_spec, row_spec, row_spec,
                  pl.BlockSpec(e_seg.shape, lambda i, j: (0, 0)),
                  pl.BlockSpec(e_ch.shape, lambda i, j: (0, 0)),
                  pl.BlockSpec((1, 1, gw), lambda i, j: (j, 0, 0)),
                  pl.BlockSpec((1, 1, gw), lambda i, j: (j, 0, 0))],
        out_specs=pl.BlockSpec((1, l, gw), lambda i, j: (i, 0, j)),
        out_shape=jax.ShapeDtypeStruct((b, l, g * gw), BF16),
        scratch_shapes=[pltpu.VMEM((2, SSD_STATE, gw), F32),
                        pltpu.VMEM((l, gw), F32)],
        compiler_params=_cparams("parallel", "parallel"),
        name="ssd_scan",
    )(xbc, xbc, xbc, proj, cum3, dt3, cumr, dtr, e_seg, e_ch, dskip, norm_w)


GLA_TILE = 256
GLA_TILES_PER_STEP = 2


def _split_dot(m, x):
    x_hi = x.astype(BF16)
    x_lo = (x - x_hi.astype(F32)).astype(BF16)
    return (jnp.dot(m, x_hi, preferred_element_type=F32) + jnp.dot(m, x_lo, preferred_element_type=F32))


def _split_dot3(a, b):
    a_hi = a.astype(BF16)
    a_lo = (a - a_hi.astype(F32)).astype(BF16)
    b_hi = b.astype(BF16)
    b_lo = (b - b_hi.astype(F32)).astype(BF16)
    return (jnp.dot(a_hi, b_hi, preferred_element_type=F32) + jnp.dot(a_lo, b_hi, preferred_element_type=F32)
            + jnp.dot(a_hi, b_lo, preferred_element_type=F32))


def _gla_kernel(q_ref, k_ref, v_ref, r_ref, lr_ref, w2f_ref, b2f_ref, w2b_ref, b2b_ref, nw_ref,
                o_ref, st_ref, y_ref, qd_ref, kt_ref, eb_ref, *, l, lc):
    c = GLA_CHUNK
    t = GLA_TILE
    n_lat, n_ctx = l // c, lc // c
    n_all = n_lat + n_ctx
    hk = q_ref.shape[2]
    scale = hk ** -0.5
    row = lax.broadcasted_iota(jnp.int32, (t, t), 0)
    colm = lax.broadcasted_iota(jnp.int32, (t, t), 1)
    same_chunk = (row // c) == (colm // c)
    masks = (same_chunk & (row >= colm), same_chunk & (row <= colm))

    def gates(off, emit, n_tiles):
        rows = n_tiles * t
        tok = pl.ds(off, rows)
        tiles = [slice(i * t, (i + 1) * t) for i in range(n_tiles)]
        per_tile = lambda f: jnp.concatenate([f(ts) for ts in tiles], axis=0)
        kf = k_ref[0, tok, :].astype(F32)
        lr_all = lr_ref[0, tok, :]
        for d in range(2):
            w2 = w2f_ref if d == 0 else w2b_ref
            b2 = b2f_ref if d == 0 else b2b_ref
            lr = lr_all[:, d * GLA_RANK:(d + 1) * GLA_RANK]
            logit = _split_dot3(lr, w2[...]) + b2[...]
            g = -_softplus(-logit) * (1.0 / GLA_NORMALIZER)
            tri = masks[d].astype(BF16)
            bcum = per_tile(lambda ts: _split_dot(tri, g[ts]))
            edge = c - 1 if d == 0 else 0
            btot = jnp.concatenate(
                [jnp.broadcast_to(bcum[j * c + edge:j * c + edge + 1, :], (c, hk)) for j in range(rows // c)],
                axis=0)
            kt_ref[d, tok, :] = (kf * jnp.exp(btot - bcum)).astype(BF16)
            ebt = jnp.exp(btot)
            for j in range(rows // c):
                eb_ref[d, pl.ds(off // c + j, 1)] = ebt[j * c:j * c + 8, :][None]
            if emit:
                qd = (q_ref[0, tok, :].astype(F32) * scale * jnp.exp(bcum)).astype(BF16)
                kd = (kf * jnp.exp(-bcum)).astype(BF16)
                qd_ref[d, tok, :] = qd

                def intra(ts):
                    a = lax.dot_general(qd[ts], kd[ts], NT_DIMS, preferred_element_type=F32)
                    a = jnp.where(masks[d], a, 0.0).astype(BF16)
                    return jnp.dot(a, v_ref[0, pl.ds(off + ts.start, t), :], preferred_element_type=F32)

                o = per_tile(intra)
                if d == 0:
                    y_ref[tok, :] = o
                else:
                    y_ref[tok, :] = y_ref[tok, :] + o

    def lat_gates(i, carry):
        gates(pl.multiple_of(i * (GLA_TILES_PER_STEP * t), GLA_TILES_PER_STEP * t), True, GLA_TILES_PER_STEP)
        return carry

    lax.fori_loop(0, l // (GLA_TILES_PER_STEP * t), lat_gates, 0)
    for i in range(lc // t):
        gates(l + i * t, False, 1)

    def step(ci, d, emit):
        tok = pl.ds(pl.multiple_of(ci * c, c), c)
        st = st_ref[d]
        if emit:
            o = lax.dot_general(qd_ref[d, tok, :], st.astype(BF16), NT_DIMS, preferred_element_type=F32)
            y_ref[tok, :] = y_ref[tok, :] + o
        st_ref[d] = st * eb_ref[d, ci][0:1, :] + lax.dot_general(
            v_ref[0, tok, :], kt_ref[d, tok, :], TN_DIMS, preferred_element_type=F32)

    def ctx_step(i, carry):
        step(n_lat + i, 0, False)
        step(n_all - 1 - i, 1, False)
        return carry

    def lat_step(i, carry):
        step(i, 0, True)
        step(n_lat - 1 - i, 1, True)
        return carry

    st_ref[...] = jnp.zeros_like(st_ref)
    lax.fori_loop(0, n_ctx, ctx_step, 0)
    lax.fori_loop(0, n_lat, lat_step, 0, unroll=2)

    rows = 256
    for t0 in range(0, l, rows):
        o = y_ref[t0:t0 + rows, :]
        ms = jnp.mean(o * o, axis=-1, keepdims=True)
        o = o * lax.rsqrt(ms + EPS) * nw_ref[...]
        o_ref[0, t0:t0 + rows, :] = (o * _silu(r_ref[0, t0:t0 + rows, :].astype(F32))).astype(o_ref.dtype)


def _gla(proj, small, w2f, b2f, w2b, b2b, norm_w, l, lc, blocks):
    b, s, _ = proj.shape
    h = GLA_HEADS
    hk = w2f.shape[1] // h
    hv = norm_w.shape[0]
    q0, k0, v0, r0 = blocks
    kern = functools.partial(_gla_kernel, l=l, lc=lc)
    return pl.pallas_call(
        kern,
        grid=(b, h),
        in_specs=[pl.BlockSpec((1, s, hk), lambda i, j: (i, 0, q0 // hk + j)),
                  pl.BlockSpec((1, s, hk), lambda i, j: (i, 0, k0 // hk + j)),
                  pl.BlockSpec((1, s, hv), lambda i, j: (i, 0, v0 // hv + j)),
                  pl.BlockSpec((1, l, hv), lambda i, j: (i, 0, r0 // hv + j)),
                  pl.BlockSpec((1, s, LANES), lambda i, j: (i, 0, 1)),
                  pl.BlockSpec((GLA_RANK, hk), lambda i, j: (0, j)),
                  pl.BlockSpec((1, hk), lambda i, j: (0, j)),
                  pl.BlockSpec((GLA_RANK, hk), lambda i, j: (0, j)),
                  pl.BlockSpec((1, hk), lambda i, j: (0, j)),
                  pl.BlockSpec((1, hv), lambda i, j: (0, 0))],
        out_specs=pl.BlockSpec((1, l, hv), lambda i, j: (i, 0, j)),
        out_shape=jax.ShapeDtypeStruct((b, l, h * hv), BF16),
        scratch_shapes=[pltpu.VMEM((2, hv, hk), F32),
                        pltpu.VMEM((l, hv), F32),
                        pltpu.VMEM((2, l, hk), BF16),
                        pltpu.VMEM((2, s, hk), BF16),
                        pltpu.VMEM((2, s // GLA_CHUNK, 8, hk), F32)],
        compiler_params=_cparams("parallel", "parallel"),
        name="gla_scan",
    )(proj, proj, proj, proj, small, w2f, b2f.reshape(1, -1), w2b, b2b.reshape(1, -1),
      norm_w.reshape(1, hv))


def _merge_kernel(a_ref, y_ref, wa_ref, wb_ref, gla_ref, glb_ref, ba_ref, bb_ref, o_ref):
    ya = jnp.dot(a_ref[0], wa_ref[...], preferred_element_type=F32)
    yb = jnp.dot(y_ref[0], wb_ref[...], preferred_element_type=F32)
    ga = jax.nn.sigmoid(gla_ref[0].astype(F32) + ba_ref[...])
    gb = jax.nn.sigmoid(glb_ref[0].astype(F32) + bb_ref[...])
    o_ref[0] = (ga * ya + gb * yb).astype(o_ref.dtype)


def _merge(a_n, y_n, w_a, w_b, proj, gl0, b_gate):
    b, l, ka = a_n.shape
    kb = y_n.shape[2]
    d = w_a.shape[1]
    bm = _pick(l, (1024, 512, 256))
    bn = 512
    nb = d // bn
    bg = b_gate.reshape(1, 2 * d)
    return pl.pallas_call(
        _merge_kernel,
        grid=(b, l // bm, nb),
        in_specs=[pl.BlockSpec((1, bm, ka), lambda i, t, j: (i, t, 0)),
                  pl.BlockSpec((1, bm, kb), lambda i, t, j: (i, t, 0)),
                  pl.BlockSpec((ka, bn), lambda i, t, j: (0, j)),
                  pl.BlockSpec((kb, bn), lambda i, t, j: (0, j)),
                  pl.BlockSpec((1, bm, bn), lambda i, t, j: (i, t, gl0 // bn + j)),
                  pl.BlockSpec((1, bm, bn), lambda i, t, j: (i, t, gl0 // bn + nb + j)),
                  pl.BlockSpec((1, bn), lambda i, t, j: (0, j)),
                  pl.BlockSpec((1, bn), lambda i, t, j: (0, nb + j))],
        out_specs=pl.BlockSpec((1, bm, bn), lambda i, t, j: (i, t, j)),
        out_shape=jax.ShapeDtypeStruct((b, l, d), BF16),
        compiler_params=_cparams("parallel", "parallel", "parallel"),
        name="branch_merge",
    )(a_n, y_n, w_a, w_b, proj, proj, bg, bg)


def _resid_kernel(m_ref, w_ref, x_ref, g_ref, o_ref):
    y = jnp.dot(m_ref[0], w_ref[...], preferred_element_type=F32)
    o_ref[0] = x_ref[0] + g_ref[0] * y


def _resid(m, w_o, x, gate):
    b, l, d = x.shape
    bm = _pick(l, (512, 256))
    bn = d
    return pl.pallas_call(
        _resid_kernel,
        grid=(b, l // bm, d // bn),
        in_specs=[pl.BlockSpec((1, bm, d), lambda i, t, j: (i, t, 0)),
                  pl.BlockSpec((d, bn), lambda i, t, j: (0, j)),
                  pl.BlockSpec((1, bm, bn), lambda i, t, j: (i, t, j)),
                  pl.BlockSpec((1, 1, bn), lambda i, t, j: (i, 0, j))],
        out_specs=pl.BlockSpec((1, bm, bn), lambda i, t, j: (i, t, j)),
        out_shape=jax.ShapeDtypeStruct((b, l, d), F32),
        compiler_params=_cparams("parallel", "parallel", "parallel"),
        name="attn_resid",
    )(m, w_o, x, gate)


def _peer_q_kernel(x_ref, nw_ref, sh_ref, sc_ref, wq_ref, h_ref, q_ref, lhs_ref):
    @pl.when(pl.program_id(2) == 0)
    def _():
        xf = x_ref[0]
        ms = jnp.mean(xf * xf, axis=-1, keepdims=True)
        y = xf * lax.rsqrt(ms + EPS) * nw_ref[...]
        hb = (y * (1.0 + sc_ref[0]) + sh_ref[0]).astype(BF16)
        lhs_ref[...] = hb
        h_ref[0] = hb

    q_ref[0] = jnp.dot(lhs_ref[...], wq_ref[...], preferred_element_type=F32)


def _peer_q(x1, norm_w, shift, scale, wq):
    b, l, d = x1.shape
    n = wq.shape[1]
    bm = _pick(l, (512, 256))
    bn = n
    return pl.pallas_call(
        _peer_q_kernel,
        grid=(b, l // bm, n // bn),
        in_specs=[pl.BlockSpec((1, bm, d), lambda i, t, j: (i, t, 0)),
                  pl.BlockSpec((1, d), lambda i, t, j: (0, 0)),
                  pl.BlockSpec((1, 1, d), lambda i, t, j: (i, 0, 0)),
                  pl.BlockSpec((1, 1, d), lambda i, t, j: (i, 0, 0)),
                  pl.BlockSpec((d, bn), lambda i, t, j: (0, j))],
        out_specs=[pl.BlockSpec((1, bm, d), lambda i, t, j: (i, t, 0)),
                   pl.BlockSpec((1, bm, bn), lambda i, t, j: (i, t, j))],
        out_shape=[jax.ShapeDtypeStruct((b, l, d), BF16),
                   jax.ShapeDtypeStruct((b, l, n), F32)],
        scratch_shapes=[pltpu.VMEM((bm, d), BF16)],
        compiler_params=_cparams("parallel", "parallel", "arbitrary"),
        name="peer_query",
    )(x1, norm_w.reshape(1, d), shift, scale, wq)


def _top_values(x, k, rows):
    sub = lax.broadcasted_iota(jnp.int32, (rows, x.shape[1]), 0)
    out = jnp.full((rows, x.shape[1]), -jnp.inf, F32)
    for i in range(k):
        m = jnp.max(x, axis=0, keepdims=True)
        out = jnp.where(sub == i, m, out)
        x = jnp.where(x == m, -jnp.inf, x)
    return out


def _peer_score_kernel(q_ref, keys_ref, c1_ref, e1_ref, s2_ref, e2_ref):
    k = PEER_TOPK
    dk = keys_ref.shape[3]
    q = q_ref[...]
    s1 = lax.dot_general(keys_ref[0, 0], q[:, 0:dk], NT_DIMS, precision=HIGHEST,
                         preferred_element_type=F32)
    s2 = lax.dot_general(keys_ref[0, 1], q[:, dk:2 * dk], NT_DIMS, precision=HIGHEST,
                         preferred_element_type=F32)
    s2_ref[0] = s2
    n = k + 1
    pad = -(-n // 8) * 8
    assert n // 2 <= 8 and n // 9 == 1
    for lc in range(q.shape[0] // LANES):
        ls = slice(lc * LANES, (lc + 1) * LANES)
        s1s, s2s = s1[:, ls], s2[:, ls]
        sv1 = _top_values(s1s, n, pad)
        sv2 = _top_values(s2s, n, pad)
        cand = jnp.concatenate(
            [sv1[0:1, :] + sv2]
            + [sv1[a:a + 1, :] + sv2[0:8, :] for a in range(1, 8)]
            + [sv1[8:pad, :] + sv2[0:1, :]], axis=0)
        cv = _top_values(cand, n, pad)
        z = jnp.sum(jnp.exp(cv[0:k, :] - cv[0:1, :]), axis=0, keepdims=True)
        thr = 0.5 * (cv[k - 1:k, :] + cv[k:k + 1, :])
        c1_ref[0, :, ls] = thr - s1s
        e1_ref[0, :, ls] = jnp.exp(s1s - sv1[0:1, :]) / z
        e2_ref[0, :, ls] = jnp.exp(s2s - sv2[0:1, :])


def _peer_scores(q, keys):
    t, n = q.shape
    h, _, nk, dk = keys.shape
    tt = _pick(t, (512, 256))
    big = jax.ShapeDtypeStruct((h, nk, t), F32)
    big_spec = pl.BlockSpec((1, nk, tt), lambda i, j: (j, 0, i))
    return pl.pallas_call(
        _peer_score_kernel,
        grid=(t // tt, h),
        in_specs=[pl.BlockSpec((tt, 2 * dk), lambda i, j: (i, j)),
                  pl.BlockSpec((1, 2, nk, dk), lambda i, j: (j, 0, 0, 0))],
        out_specs=[big_spec, big_spec, big_spec, big_spec],
        out_shape=[big, big, big, big],
        compiler_params=_cparams("parallel", "parallel"),
        name="peer_scores",
    )(q, keys)


PEER_EXPERT_BLOCK = 1024
GATE_ROWS = 8


def _peer_expert_kernel(h_ref, u_ref, vt_ref, c1_ref, e1_ref, s2_ref, e2_ref,
                        o_ref, ht_ref, w_ref, c8_ref, e8_ref, *, n_sub):
    j = pl.program_id(1)
    nh, nk, bm = s2_ref.shape
    sub = 8

    @pl.when(j == 0)
    def _():
        o_ref[...] = jnp.zeros_like(o_ref)
        ht_ref[...] = h_ref[...].astype(F32).T.astype(BF16)

    i1_0 = pl.multiple_of(j * n_sub, n_sub)
    for hd in range(nh):
        cgrp = c1_ref[hd, pl.ds(i1_0, n_sub), :]
        egrp = e1_ref[hd, pl.ds(i1_0, n_sub), :]
        for sb in range(n_sub):
            r0 = (hd * n_sub + sb) * sub
            c8_ref[r0:r0 + sub, :] = jnp.broadcast_to(cgrp[sb:sb + 1, :], (sub, bm))
            e8_ref[r0:r0 + sub, :] = jnp.broadcast_to(egrp[sb:sb + 1, :], (sub, bm))

    act = jnp.dot(u_ref[...], ht_ref[...], preferred_element_type=F32)

    n_part = GATE_ROWS // sub

    for lc in range(bm // LANES):
        ls = slice(lc * LANES, (lc + 1) * LANES)
        for rc in range(nk // GATE_ROWS):
            base = rc * GATE_ROWS
            accs = [[jnp.zeros((sub, LANES), F32) for _ in range(n_part)] for _ in range(n_sub)]
            for hd in range(nh):
                s2v = [s2_ref[hd, base + k * sub:base + (k + 1) * sub, ls] for k in range(n_part)]
                e2v = [e2_ref[hd, base + k * sub:base + (k + 1) * sub, ls] for k in range(n_part)]
                for sb in range(n_sub):
                    r0 = (hd * n_sub + sb) * sub
                    cut = c8_ref[r0:r0 + sub, ls]
                    e1v = e8_ref[r0:r0 + sub, ls]
                    for k in range(n_part):
                        accs[sb][k] = accs[sb][k] + jnp.where(s2v[k] >= cut, e2v[k], 0.0) * e1v
            for sb in range(n_sub):
                for k in range(n_part):
                    r0 = sb * nk + base + k * sub
                    w_ref[r0:r0 + sub, ls] = accs[sb][k]

    gelu = 0.5 * act * (1.0 + lax.erf(act * (2.0 ** -0.5)))
    pmat = (gelu * w_ref[...]).astype(BF16)
    o_ref[...] += jnp.dot(vt_ref[...], pmat, preferred_element_type=F32)


def _peer_experts(h2, u, vt, c1, e1, s2, e2):
    t, d = h2.shape
    e = u.shape[0]
    nh, nk, _ = s2.shape
    bm = _pick(t, (512, 256))
    eb = PEER_EXPERT_BLOCK
    n_sub = eb // nk
    kern = functools.partial(_peer_expert_kernel, n_sub=n_sub)
    big_spec = pl.BlockSpec((nh, nk, bm), lambda i, j: (0, 0, i))
    return pl.pallas_call(
        kern,
        grid=(t // bm, e // eb),
        in_specs=[pl.BlockSpec((bm, d), lambda i, j: (i, 0)),
                  pl.BlockSpec((eb, d), lambda i, j: (j, 0)),
                  pl.BlockSpec((d, eb), lambda i, j: (0, j)),
                  big_spec, big_spec, big_spec, big_spec],
        out_specs=pl.BlockSpec((d, bm), lambda i, j: (0, i)),
        out_shape=jax.ShapeDtypeStruct((d, t), F32),
        scratch_shapes=[pltpu.VMEM((d, bm), BF16),
                        pltpu.VMEM((eb, bm), F32),
                        pltpu.VMEM((nh * n_sub * 8, bm), F32),
                        pltpu.VMEM((nh * n_sub * 8, bm), F32)],
        compiler_params=_cparams("parallel", "arbitrary"),
        name="peer_experts",
    )(h2, u, vt, c1, e1, s2, e2)


def _final_kernel(x_ref, yt_ref, g_ref, w_ref, o_ref):
    xf = x_ref[0] + g_ref[0] * yt_ref[...].T
    ms = jnp.mean(xf * xf, axis=-1, keepdims=True)
    o_ref[0] = xf * lax.rsqrt(ms + EPS) * w_ref[...]


def _final(x1, yt, gate, w):
    b, l, d = x1.shape
    bm = _pick(l, (512, 256))
    nt = l // bm
    return pl.pallas_call(
        _final_kernel,
        grid=(b, l // bm),
        in_specs=[pl.BlockSpec((1, bm, d), lambda i, t: (i, t, 0)),
                  pl.BlockSpec((d, bm), lambda i, t: (0, i * nt + t)),
                  pl.BlockSpec((1, 1, d), lambda i, t: (i, 0, 0)),
                  pl.BlockSpec((1, d), lambda i, t: (0, 0))],
        out_specs=pl.BlockSpec((1, bm, d), lambda i, t: (i, t, 0)),
        out_shape=jax.ShapeDtypeStruct((b, l, d), F32),
        compiler_params=_cparams("parallel", "parallel"),
        name="final_norm",
    )(x1, yt, gate, w.reshape(1, d))


def _layer(x, ctx, mod_x, mod_c, p):
    b, l, d = x.shape
    lc = ctx.shape[1]
    s = l + lc
    dk = p['w_lr2_f'].shape[1]
    dv = p['w_gla_out'].shape[0]
    di = p['w_ssd_out'].shape[0]
    bc = SSD_GROUPS * SSD_STATE
    n_ssd_heads = p['a_log_f'].shape[0]

    sizes = (dk, dk, dv, dv, GLA_RANK, GLA_RANK, di, di + 2 * bc, n_ssd_heads, n_ssd_heads, 2 * d)
    offs = [0]
    for sz in sizes:
        offs.append(offs[-1] + sz)
    w_in = p['w_in']
    seg = lambda i: w_in[:, offs[i]:offs[i + 1]]
    w_main = jnp.concatenate([seg(0), seg(1), seg(2), seg(3), seg(6), seg(7), seg(10)], axis=1).astype(BF16)
    n_small = 2 * LANES
    w_small = jnp.concatenate(
        [seg(8), seg(9), seg(4), seg(5),
         jnp.zeros((d, n_small - 2 * n_ssd_heads - 2 * GLA_RANK), w_in.dtype)], axis=1).astype(BF16)
    q0, k0, v0, r0 = 0, dk, 2 * dk, 2 * dk + dv
    z0 = r0 + dv
    xbc0 = z0 + di
    gl0 = xbc0 + di + 2 * bc

    shift1 = jnp.stack([mod_x[0], jnp.broadcast_to(mod_c[0], (b, d))], axis=1).reshape(b, 2, 1, d)
    scale1 = jnp.stack([mod_x[1], jnp.broadcast_to(mod_c[1], (b, d))], axis=1).reshape(b, 2, 1, d)
    h = _norm_mod(x, ctx, p['norm1_w'], shift1, scale1)

    h2d = h.reshape(b * s, d)
    proj = _matmul(h2d, w_main, BF16, "in_proj").reshape(b, s, -1)
    small = _matmul(h2d, w_small, F32, "in_proj_gates").reshape(b, s, n_small)

    xbc = _conv(proj, xbc0 // 512, di + 2 * bc, p['conv_w'], p['conv_b'], l, lc)
    bias = jnp.concatenate([p['dt_bias_f'], p['dt_bias_b']]).reshape(1, LANES).astype(F32)
    a_neg = -jnp.exp(jnp.concatenate([p['a_log_f'], p['a_log_b']]).astype(F32)).reshape(1, LANES)
    dtr, cumr, dt3, cum3 = _ssd_prep(small, bias, a_neg)
    g, hpg, c = SSD_GROUPS, SSD_HPG, SSD_CHUNK
    per_dir = lambda t: t.reshape(b, 2, g, s // c, hpg, c)
    dskip = jnp.repeat(p['d_skip'].astype(F32), SSD_HEADDIM).reshape(g, 1, hpg * SSD_HEADDIM)
    ssd_nw = p['ssd_norm_w'].astype(F32).reshape(g, 1, hpg * SSD_HEADDIM)
    y_n = _ssd(xbc, proj, z0 // 512, _split3_cols(cum3, b, s), _split3_cols(dt3, b, s),
               per_dir(cumr), per_dir(dtr), dskip, ssd_nw, l, lc)

    a_n = _gla(proj, small, p['w_lr2_f'], p['b_lr_f'], p['w_lr2_b'], p['b_lr_b'], p['gla_norm_w'],
               l, lc, (q0, k0, v0, r0))

    m = _merge(a_n, y_n, p['w_gla_out'].astype(BF16), p['w_ssd_out'].astype(BF16), proj, gl0,
               p['b_gate'])
    return _resid(m, p['w_o'].astype(BF16), x, mod_x[2].reshape(b, 1, d))


def _peer(x1, mod_x, norm_w, wq, keys, u, v):
    b, l, d = x1.shape
    h2, q = _peer_q(x1, norm_w, mod_x[3].reshape(b, 1, d), mod_x[4].reshape(b, 1, d), wq.astype(BF16))
    c1, e1, s2, e2 = _peer_scores(q.reshape(b * l, -1), keys)
    return _peer_experts(h2.reshape(b * l, d), u.astype(BF16), v.astype(BF16).T, c1, e1, s2, e2)


def kernel(x, c, ctx, c_ctx, w_ada, b_ada, norm1_w, w_in, b_gate, w_lr2_f, b_lr_f, w_lr2_b, b_lr_b,
           gla_norm_w, w_gla_out, conv_w, conv_b, a_log_f, a_log_b, dt_bias_f, dt_bias_b, d_skip,
           ssd_norm_w, w_ssd_out, w_o, norm2_w, peer_wq, peer_keys, peer_u, peer_v, final_norm_w):
    b, l, d = x.shape
    depth = w_in.shape[0]
    assert depth == 1, "context-stream update for deeper stacks is not implemented"
    layer = 0
    rows = -(-(b + 1) // 8) * 8
    c_all = jnp.concatenate([c, c_ctx[None, :], jnp.zeros((rows - b - 1, d), c.dtype)], axis=0)
    mod = _ada(c_all, w_ada[layer], b_ada[layer])
    mod_x = [mod[:b, i * d:(i + 1) * d] for i in range(N_MOD)]
    mod_c = [mod[b, i * d:(i + 1) * d] for i in range(N_MOD)]
    p = {
        'norm1_w': norm1_w[layer], 'w_in': w_in[layer], 'b_gate': b_gate[layer],
        'w_lr2_f': w_lr2_f[layer], 'b_lr_f': b_lr_f[layer],
        'w_lr2_b': w_lr2_b[layer], 'b_lr_b': b_lr_b[layer],
        'gla_norm_w': gla_norm_w[layer], 'w_gla_out': w_gla_out[layer],
        'conv_w': conv_w[layer], 'conv_b': conv_b[layer],
        'a_log_f': a_log_f[layer], 'a_log_b': a_log_b[layer],
        'dt_bias_f': dt_bias_f[layer], 'dt_bias_b': dt_bias_b[layer],
        'd_skip': d_skip[layer], 'ssd_norm_w': ssd_norm_w[layer],
        'w_ssd_out': w_ssd_out[layer], 'w_o': w_o[layer],
    }
    x1 = _layer(x, ctx, mod_x, mod_c, p)
    y = _peer(x1, mod_x, norm2_w[layer], peer_wq[layer], peer_keys[layer], peer_u[layer], peer_v[layer])
    return _final(x1, y, mod_x[5].reshape(b, 1, d), final_norm_w)
```

```python
import functools

import jax
import jax.numpy as jnp
from jax import lax
from jax.experimental import pallas as pl
from jax.experimental.pallas import tpu as pltpu

F32 = jnp.float32
BF16 = jnp.bfloat16
EPS = 1e-6
HIGHEST = lax.Precision.HIGHEST

N_MOD = 6
GRID_W = 64
GLA_HEADS = 4
GLA_RANK = 16
GLA_NORMALIZER = 16.0
GLA_CHUNK = 64
SSD_HEADDIM = 64
SSD_GROUPS = 8
SSD_HPG = 8
SSD_STATE = 128
SSD_CHUNK = 128
PEER_HEADS = 8
PEER_NKEYS = 128
PEER_TOPK = 16

VMEM_LIMIT_BYTES = 56 * 1024 * 1024
LANES = 128

NT_DIMS = (((1,), (1,)), ((), ()))
TN_DIMS = (((0,), (0,)), ((), ()))


def _cparams(*sem, flags=None):
    return pltpu.CompilerParams(dimension_semantics=sem, vmem_limit_bytes=VMEM_LIMIT_BYTES, flags=flags)


def _pick(n, options):
    for o in options:
        if n % o == 0:
            return o
    raise ValueError(f"no tile in {options} divides {n}")


def _softplus(x):
    return jnp.maximum(x, 0.0) + jnp.log1p(jnp.exp(-jnp.abs(x)))


def _silu(x):
    return x * jax.nn.sigmoid(x)


def _ada_kernel(c_ref, w_ref, b_ref, o_ref):
    a = _silu(c_ref[...])
    o_ref[...] = jnp.dot(a, w_ref[...], precision=HIGHEST, preferred_element_type=F32) + b_ref[...]


def _ada(c_all, w_ada, b_ada):
    m, d = c_all.shape
    n = w_ada.shape[1]
    bn = _pick(n, (1024, 512, 256, 128))
    return pl.pallas_call(
        _ada_kernel,
        grid=(n // bn,),
        in_specs=[pl.BlockSpec((m, d), lambda j: (0, 0)),
                  pl.BlockSpec((d, bn), lambda j: (0, j)),
                  pl.BlockSpec((1, bn), lambda j: (0, j))],
        out_specs=pl.BlockSpec((m, bn), lambda j: (0, j)),
        out_shape=jax.ShapeDtypeStruct((m, n), F32),
        compiler_params=_cparams("parallel"),
        name="ada_mod",
    )(c_all, w_ada, b_ada.reshape(1, n))


def _norm_mod_kernel(x_ref, ctx_ref, w_ref, sh_ref, sc_ref, o_ref, *, n_lat_tiles):
    j = pl.program_id(1)

    def emit(src):
        xf = src[0]
        ms = jnp.mean(xf * xf, axis=-1, keepdims=True)
        y = xf * lax.rsqrt(ms + EPS) * w_ref[...]
        o_ref[0] = (y * (1.0 + sc_ref[0, 0]) + sh_ref[0, 0]).astype(o_ref.dtype)

    @pl.when(j < n_lat_tiles)
    def _():
        emit(x_ref)

    @pl.when(j >= n_lat_tiles)
    def _():
        emit(ctx_ref)


def _norm_mod(x, ctx, w, shift, scale):
    b, l, d = x.shape
    lc = ctx.shape[1]
    tn = _pick(lc, (256, 128))
    nl, nc = l // tn, lc // tn
    kern = functools.partial(_norm_mod_kernel, n_lat_tiles=nl)
    return pl.pallas_call(
        kern,
        grid=(b, nl + nc),
        in_specs=[pl.BlockSpec((1, tn, d), lambda i, j: (i, jnp.minimum(j, nl - 1), 0)),
                  pl.BlockSpec((1, tn, d), lambda i, j: (i, jnp.maximum(j - nl, 0), 0)),
                  pl.BlockSpec((1, d), lambda i, j: (0, 0)),
                  pl.BlockSpec((1, 1, 1, d), lambda i, j: (i, j // nl, 0, 0)),
                  pl.BlockSpec((1, 1, 1, d), lambda i, j: (i, j // nl, 0, 0))],
        out_specs=pl.BlockSpec((1, tn, d), lambda i, j: (i, j, 0)),
        out_shape=jax.ShapeDtypeStruct((b, l + lc, d), BF16),
        compiler_params=_cparams("parallel", "parallel"),
        name="norm_mod",
    )(x, ctx, w.reshape(1, d), shift, scale)


def _matmul_kernel(a_ref, b_ref, o_ref):
    o_ref[...] = jnp.dot(a_ref[...], b_ref[...], preferred_element_type=F32).astype(o_ref.dtype)


def _matmul(a, b, out_dtype, name):
    m, k = a.shape
    n = b.shape[1]
    bm = _pick(m, (2048, 1024, 512, 256))
    bn = _pick(n, (1024, 512, 256))
    return pl.pallas_call(
        _matmul_kernel,
        grid=(m // bm, n // bn),
        in_specs=[pl.BlockSpec((bm, k), lambda i, j: (i, 0)),
                  pl.BlockSpec((k, bn), lambda i, j: (0, j))],
        out_specs=pl.BlockSpec((bm, bn), lambda i, j: (i, j)),
        out_shape=jax.ShapeDtypeStruct((m, n), out_dtype),
        compiler_params=_cparams("parallel", "parallel"),
        name=name,
    )(a, b)


CONV_TOK = 256
CTX_HALO = 8
LAT_HALO = GRID_W + 8


def _conv_kernel(x_ref, w_ref, b_ref, o_ref, lat_ref, ctx_ref, *, l, lc):
    tc = x_ref.shape[2]
    lat_ref[0:LAT_HALO, :] = jnp.zeros((LAT_HALO, tc), F32)
    lat_ref[LAT_HALO + l:LAT_HALO + l + LAT_HALO, :] = jnp.zeros((LAT_HALO, tc), F32)
    ctx_ref[0:CTX_HALO, :] = jnp.zeros((CTX_HALO, tc), F32)
    ctx_ref[CTX_HALO + lc:CTX_HALO + lc + CTX_HALO, :] = jnp.zeros((CTX_HALO, tc), F32)
    lat_ref[LAT_HALO:LAT_HALO + l, :] = x_ref[0, 0:l, :].astype(F32)
    ctx_ref[CTX_HALO:CTX_HALO + lc, :] = x_ref[0, l:l + lc, :].astype(F32)

    bias = b_ref[...]
    col = lax.broadcasted_iota(jnp.int32, (CONV_TOK, tc), 0) % GRID_W
    not_first = col != 0
    not_last = col != GRID_W - 1

    for t0 in range(0, l, CONV_TOK):
        acc = jnp.zeros((CONV_TOK, tc), F32)
        for kw in range(3):
            part = jnp.zeros((CONV_TOK, tc), F32)
            for kh in range(3):
                start = LAT_HALO + t0 + GRID_W * (kh - 1) + (kw - 1)
                part = part + w_ref[kh * 3 + kw:kh * 3 + kw + 1, :] * lat_ref[start:start + CONV_TOK, :]
            if kw == 0:
                part = jnp.where(not_first, part, 0.0)
            elif kw == 2:
                part = jnp.where(not_last, part, 0.0)
            acc = acc + part
        o_ref[0, t0:t0 + CONV_TOK, :] = _silu(acc + bias).astype(o_ref.dtype)

    ctx_tok = min(CONV_TOK, lc)
    for t0 in range(0, lc, ctx_tok):
        acc = jnp.zeros((ctx_tok, tc), F32)
        for kw in range(3):
            start = CTX_HALO + t0 + (kw - 1)
            acc = acc + w_ref[3 + kw:4 + kw, :] * ctx_ref[start:start + ctx_tok, :]
        o_ref[0, l + t0:l + t0 + ctx_tok, :] = _silu(acc + bias).astype(o_ref.dtype)


def _conv(proj, col_block0, n_ch, conv_w, conv_b, l, lc):
    b, s, _ = proj.shape
    tc = 512
    kern = functools.partial(_conv_kernel, l=l, lc=lc)
    return pl.pallas_call(
        kern,
        grid=(b, n_ch // tc),
        in_specs=[pl.BlockSpec((1, s, tc), lambda i, j: (i, 0, col_block0 + j)),
                  pl.BlockSpec((9, tc), lambda i, j: (0, j)),
                  pl.BlockSpec((1, tc), lambda i, j: (0, j))],
        out_specs=pl.BlockSpec((1, s, tc), lambda i, j: (i, 0, j)),
        out_shape=jax.ShapeDtypeStruct((b, s, n_ch), BF16),
        scratch_shapes=[pltpu.VMEM((l + 2 * LAT_HALO, tc), F32),
                        pltpu.VMEM((lc + 2 * CTX_HALO, tc), F32)],
        compiler_params=_cparams("parallel", "parallel"),
        name="dwconv_silu",
    )(proj, conv_w.reshape(9, n_ch), conv_b.reshape(1, n_ch))


def _bf16_terms(x):
    hi = x.astype(BF16)
    r1 = x - hi.astype(F32)
    mid = r1.astype(BF16)
    lo = (r1 - mid.astype(F32)).astype(BF16)
    return hi, mid, lo


def _ssd_prep_kernel(x_ref, bias_ref, a_ref, dtr_ref, cumr_ref, dt3_ref, cum3_ref):
    c = SSD_CHUNK
    s = x_ref.shape[1]
    n_dg = LANES // SSD_HPG
    row = lax.broadcasted_iota(jnp.int32, (c, c), 0)
    colm = lax.broadcasted_iota(jnp.int32, (c, c), 1)
    tri_f = (row >= colm).astype(F32)
    tri_b = (row <= colm).astype(F32)
    is_fwd = lax.broadcasted_iota(jnp.int32, (c, LANES), 1) < (LANES // 2)
    for t0 in range(0, s, c):
        dt = _softplus(x_ref[0, t0:t0 + c, :] + bias_ref[...])
        dta = dt * a_ref[...]
        cf = jnp.dot(tri_f, dta, precision=HIGHEST, preferred_element_type=F32)
        cb = jnp.dot(tri_b, dta, precision=HIGHEST, preferred_element_type=F32)
        cum = jnp.where(is_fwd, cf, cb)
        dt_t, cum_t = dt.T, cum.T
        for dg in range(n_dg):
            dtr_ref[0, dg, t0 // c] = dt_t[dg * SSD_HPG:(dg + 1) * SSD_HPG, :]
            cumr_ref[0, dg, t0 // c] = cum_t[dg * SSD_HPG:(dg + 1) * SSD_HPG, :]
        for term, (dt_t, cum_t) in enumerate(zip(_bf16_terms(dt), _bf16_terms(cum))):
            dt3_ref[0, term, t0:t0 + c, :] = dt_t
            cum3_ref[0, term, t0:t0 + c, :] = cum_t


def _ssd_prep(small, bias, a):
    b, s, _ = small.shape
    n_dg = LANES // SSD_HPG
    rows = jax.ShapeDtypeStruct((b, n_dg, s // SSD_CHUNK, SSD_HPG, SSD_CHUNK), F32)
    out3 = jax.ShapeDtypeStruct((b, 3, s, LANES), BF16)
    spec = pl.BlockSpec((1, s, LANES), lambda i: (i, 0, 0))
    rows_spec = pl.BlockSpec((1, n_dg, s // SSD_CHUNK, SSD_HPG, SSD_CHUNK), lambda i: (i, 0, 0, 0, 0))
    spec3 = pl.BlockSpec((1, 3, s, LANES), lambda i: (i, 0, 0, 0))
    return pl.pallas_call(
        _ssd_prep_kernel,
        grid=(b,),
        in_specs=[spec,
                  pl.BlockSpec((1, LANES), lambda i: (0, 0)),
                  pl.BlockSpec((1, LANES), lambda i: (0, 0))],
        out_specs=[rows_spec, rows_spec, spec3, spec3],
        out_shape=[rows, rows, out3, out3],
        compiler_params=_cparams("parallel"),
        name="ssd_prep",
    )(small, bias, a)


def _ssd_kernel(xs_ref, bm_ref, cm_ref, z_ref, cum3_ref, dt3_ref, cumr_ref, dtr_ref, e_seg_ref, e_ch_ref,
                dsk_ref, nw_ref, o_ref, s_ref, y_ref, *, l, lc):
    c = SSD_CHUNK
    p = SSD_HEADDIM
    nh = SSD_HPG
    n_lat, n_ctx = l // c, lc // c
    n_all = n_lat + n_ctx
    row = lax.broadcasted_iota(jnp.int32, (c, c), 0)
    colm = lax.broadcasted_iota(jnp.int32, (c, c), 1)
    masks = (row >= colm, row <= colm)

    def chunk(ci, d, emit):
        tok = pl.ds(pl.multiple_of(ci * c, c), c)
        last = c - 1 if d == 0 else 0
        x = xs_ref[0, tok, :]
        bmat = bm_ref[0, tok, :]
        cum3 = cum3_ref[0, d, tok, :]
        cum_ch = jnp.dot(cum3, e_ch_ref[...], preferred_element_type=F32)
        dt_ch = jnp.dot(dt3_ref[0, d, tok, :], e_ch_ref[...], preferred_element_type=F32)
        cum_end = cum_ch[last:last + 1, :]
        state = s_ref[d]
        if emit:
            cmat = cm_ref[0, tok, :]
            cumr = cumr_ref[0, d, 0, ci]
            dtr = dtr_ref[0, d, 0, ci]
            cum_seg = jnp.dot(cum3, e_seg_ref[...], preferred_element_type=F32)
            cb = lax.dot_general(cmat, bmat, NT_DIMS, preferred_element_type=F32)
            ys = []
            for h in range(nh):
                seg = cum_seg[:, h * c:(h + 1) * c] - cumr[h:h + 1, :]
                decay = jnp.exp(jnp.where(masks[d], seg, -jnp.inf))
                m = (cb * decay * dtr[h:h + 1, :]).astype(BF16)
                ys.append(jnp.dot(m, x[:, h * p:(h + 1) * p], preferred_element_type=F32))
            y = jnp.concatenate(ys, axis=1)
            y = y + jnp.dot(cmat, state.astype(BF16), preferred_element_type=F32) * jnp.exp(cum_ch)
            y_ref[tok, :] = y_ref[tok, :] + y
        xw = (x.astype(F32) * (jnp.exp(cum_end - cum_ch) * dt_ch)).astype(BF16)
        s_ref[d] = state * jnp.exp(cum_end) + lax.dot_general(bmat, xw, TN_DIMS,
                                                              preferred_element_type=F32)

    def ctx_step(i, carry):
        chunk(n_lat + i, 0, False)
        chunk(n_all - 1 - i, 1, False)
        return carry

    def lat_step(i, carry):
        chunk(i, 0, True)
        chunk(n_lat - 1 - i, 1, True)
        return carry

    rows = 256
    for t0 in range(0, l, rows):
        y_ref[t0:t0 + rows, :] = dsk_ref[0] * xs_ref[0, t0:t0 + rows, :].astype(F32)
    s_ref[...] = jnp.zeros_like(s_ref)
    lax.fori_loop(0, n_ctx, ctx_step, 0)
    lax.fori_loop(0, n_lat, lat_step, 0, unroll=2)

    rows = 256
    for t0 in range(0, l, rows):
        y = y_ref[t0:t0 + rows, :] * _silu(z_ref[0, t0:t0 + rows, :].astype(F32))
        ms = jnp.mean(y * y, axis=-1, keepdims=True)
        o_ref[0, t0:t0 + rows, :] = (y * lax.rsqrt(ms + EPS) * nw_ref[0]).astype(o_ref.dtype)


SPLIT_K = 32


def _split3_cols(t3, b, s):
    g, hpg = SSD_GROUPS, SSD_HPG
    parts = jnp.concatenate([t3, jnp.zeros_like(t3[:, :1])], axis=1)
    parts = parts.reshape(b, SPLIT_K // hpg, s, 2, g, hpg).transpose(0, 4, 3, 2, 1, 5)
    return parts.reshape(b * g, 2, s, SPLIT_K)


def _expansion(width):
    k = jnp.arange(SPLIT_K)[:, None]
    n = jnp.arange(SSD_HPG * width)[None, :]
    return ((k % SSD_HPG == n // width) & (k < 3 * SSD_HPG)).astype(BF16)


def _ssd(xbc, proj, z_block0, cum3, dt3, cumr, dtr, dskip, norm_w, l, lc):
    b, s, _ = xbc.shape
    g = SSD_GROUPS
    gw = SSD_HPG * SSD_HEADDIM
    nchunks = s // SSD_CHUNK
    xs_blocks = g * gw // SSD_STATE
    e_seg = _expansion(SSD_CHUNK)
    e_ch = _expansion(SSD_HEADDIM)
    kern = functools.partial(_ssd_kernel, l=l, lc=lc)
    col_spec = pl.BlockSpec((1, 2, s, SPLIT_K), lambda i, j: (i * g + j, 0, 0, 0))
    row_spec = pl.BlockSpec((1, 2, 1, nchunks, SSD_HPG, SSD_CHUNK), lambda i, j: (i, 0, j, 0, 0, 0))
    return pl.pallas_call(
        kern,
        grid=(b, g),
        in_specs=[pl.BlockSpec((1, s, gw), lambda i, j: (i, 0, j)),
                  pl.BlockSpec((1, s, SSD_STATE), lambda i, j: (i, 0, xs_blocks + j)),
                  pl.BlockSpec((1, s, SSD_STATE), lambda i, j: (i, 0, xs_blocks + g + j)),
                  pl.BlockSpec((1, l, gw), lambda i, j: (i, 0, z_block0 + j)),
                  col_spec, col_spec, row_spec, row_spec,
                  pl.BlockSpec(e_seg.shape, lambda i, j: (0, 0)),
                  pl.BlockSpec(e_ch.shape, lambda i, j: (0, 0)),
                  pl.BlockSpec((1, 1, gw), lambda i, j: (j, 0, 0)),
                  pl.BlockSpec((1, 1, gw), lambda i, j: (j, 0, 0))],
        out_specs=pl.BlockSpec((1, l, gw), lambda i, j: (i, 0, j)),
        out_shape=jax.ShapeDtypeStruct((b, l, g * gw), BF16),
        scratch_shapes=[pltpu.VMEM((2, SSD_STATE, gw), F32),
                        pltpu.VMEM((l, gw), F32)],
        compiler_params=_cparams("parallel", "parallel"),
        name="ssd_scan",
    )(xbc, xbc, xbc, proj, cum3, dt3, cumr, dtr, e_seg, e_ch, dskip, norm_w)


GLA_TILE = 256
GLA_TILES_PER_STEP = 2


def _split_dot(m, x):
    x_hi = x.astype(BF16)
    x_lo = (x - x_hi.astype(F32)).astype(BF16)
    return (jnp.dot(m, x_hi, preferred_element_type=F32) + jnp.dot(m, x_lo, preferred_element_type=F32))


def _split_dot3(a, b):
    a_hi = a.astype(BF16)
    a_lo = (a - a_hi.astype(F32)).astype(BF16)
    b_hi = b.astype(BF16)
    b_lo = (b - b_hi.astype(F32)).astype(BF16)
    return (jnp.dot(a_hi, b_hi, preferred_element_type=F32) + jnp.dot(a_lo, b_hi, preferred_element_type=F32)
            + jnp.dot(a_hi, b_lo, preferred_element_type=F32))


def _gla_kernel(q_ref, k_ref, v_ref, r_ref, lr_ref, w2f_ref, b2f_ref, w2b_ref, b2b_ref, nw_ref,
                o_ref, st_ref, y_ref, qd_ref, kt_ref, eb_ref, *, l, lc):
    c = GLA_CHUNK
    t = GLA_TILE
    n_lat, n_ctx = l // c, lc // c
    n_all = n_lat + n_ctx
    hk = q_ref.shape[2]
    scale = hk ** -0.5
    row = lax.broadcasted_iota(jnp.int32, (t, t), 0)
    colm = lax.broadcasted_iota(jnp.int32, (t, t), 1)
    same_chunk = (row // c) == (colm // c)
    masks = (same_chunk & (row >= colm), same_chunk & (row <= colm))

    def gates(off, emit, n_tiles):
        rows = n_tiles * t
        tok = pl.ds(off, rows)
        tiles = [slice(i * t, (i + 1) * t) for i in range(n_tiles)]
        per_tile = lambda f: jnp.concatenate([f(ts) for ts in tiles], axis=0)
        kf = k_ref[0, tok, :].astype(F32)
        lr_all = lr_ref[0, tok, :]
        for d in range(2):
            w2 = w2f_ref if d == 0 else w2b_ref
            b2 = b2f_ref if d == 0 else b2b_ref
            lr = lr_all[:, d * GLA_RANK:(d + 1) * GLA_RANK]
            logit = _split_dot3(lr, w2[...]) + b2[...]
            g = -_softplus(-logit) * (1.0 / GLA_NORMALIZER)
            tri = masks[d].astype(BF16)
            bcum = per_tile(lambda ts: _split_dot(tri, g[ts]))
            edge = c - 1 if d == 0 else 0
            btot = jnp.concatenate(
                [jnp.broadcast_to(bcum[j * c + edge:j * c + edge + 1, :], (c, hk)) for j in range(rows // c)],
                axis=0)
            kt_ref[d, tok, :] = (kf * jnp.exp(btot - bcum)).astype(BF16)
            ebt = jnp.exp(btot)
            for j in range(rows // c):
                eb_ref[d, pl.ds(off // c + j, 1)] = ebt[j * c:j * c + 8, :][None]
            if emit:
                qd = (q_ref[0, tok, :].astype(F32) * scale * jnp.exp(bcum)).astype(BF16)
                kd = (kf * jnp.exp(-bcum)).astype(BF16)
                qd_ref[d, tok, :] = qd

                def intra(ts):
                    a = lax.dot_general(qd[ts], kd[ts], NT_DIMS, preferred_element_type=F32)
                    a = jnp.where(masks[d], a, 0.0).astype(BF16)
                    return jnp.dot(a, v_ref[0, pl.ds(off + ts.start, t), :], preferred_element_type=F32)

                o = per_tile(intra)
                if d == 0:
                    y_ref[tok, :] = o
                else:
                    y_ref[tok, :] = y_ref[tok, :] + o

    def lat_gates(i, carry):
        gates(pl.multiple_of(i * (GLA_TILES_PER_STEP * t), GLA_TILES_PER_STEP * t), True, GLA_TILES_PER_STEP)
        return carry

    lax.fori_loop(0, l // (GLA_TILES_PER_STEP * t), lat_gates, 0)
    for i in range(lc // t):
        gates(l + i * t, False, 1)

    def step(ci, d, emit):
        tok = pl.ds(pl.multiple_of(ci * c, c), c)
        st = st_ref[d]
        if emit:
            o = lax.dot_general(qd_ref[d, tok, :], st.astype(BF16), NT_DIMS, preferred_element_type=F32)
            y_ref[tok, :] = y_ref[tok, :] + o
        st_ref[d] = st * eb_ref[d, ci][0:1, :] + lax.dot_general(
            v_ref[0, tok, :], kt_ref[d, tok, :], TN_DIMS, preferred_element_type=F32)

    def ctx_step(i, carry):
        step(n_lat + i, 0, False)
        step(n_all - 1 - i, 1, False)
        return carry

    def lat_step(i, carry):
        step(i, 0, True)
        step(n_lat - 1 - i, 1, True)
        return carry

    st_ref[...] = jnp.zeros_like(st_ref)
    lax.fori_loop(0, n_ctx, ctx_step, 0)
    lax.fori_loop(0, n_lat, lat_step, 0, unroll=2)

    rows = 256
    for t0 in range(0, l, rows):
        o = y_ref[t0:t0 + rows, :]
        ms = jnp.mean(o * o, axis=-1, keepdims=True)
        o = o * lax.rsqrt(ms + EPS) * nw_ref[...]
        o_ref[0, t0:t0 + rows, :] = (o * _silu(r_ref[0, t0:t0 + rows, :].astype(F32))).astype(o_ref.dtype)


def _gla(proj, small, w2f, b2f, w2b, b2b, norm_w, l, lc, blocks):
    b, s, _ = proj.shape
    h = GLA_HEADS
    hk = w2f.shape[1] // h
    hv = norm_w.shape[0]
    q0, k0, v0, r0 = blocks
    assert l % (GLA_TILES_PER_STEP * GLA_TILE) == 0 and lc % GLA_TILE == 0
    kern = functools.partial(_gla_kernel, l=l, lc=lc)
    return pl.pallas_call(
        kern,
        grid=(b, h),
        in_specs=[pl.BlockSpec((1, s, hk), lambda i, j: (i, 0, q0 // hk + j)),
                  pl.BlockSpec((1, s, hk), lambda i, j: (i, 0, k0 // hk + j)),
                  pl.BlockSpec((1, s, hv), lambda i, j: (i, 0, v0 // hv + j)),
                  pl.BlockSpec((1, l, hv), lambda i, j: (i, 0, r0 // hv + j)),
                  pl.BlockSpec((1, s, LANES), lambda i, j: (i, 0, 1)),
                  pl.BlockSpec((GLA_RANK, hk), lambda i, j: (0, j)),
                  pl.BlockSpec((1, hk), lambda i, j: (0, j)),
                  pl.BlockSpec((GLA_RANK, hk), lambda i, j: (0, j)),
                  pl.BlockSpec((1, hk), lambda i, j: (0, j)),
                  pl.BlockSpec((1, hv), lambda i, j: (0, 0))],
        out_specs=pl.BlockSpec((1, l, hv), lambda i, j: (i, 0, j)),
        out_shape=jax.ShapeDtypeStruct((b, l, h * hv), BF16),
        scratch_shapes=[pltpu.VMEM((2, hv, hk), F32),
                        pltpu.VMEM((l, hv), F32),
                        pltpu.VMEM((2, l, hk), BF16),
                        pltpu.VMEM((2, s, hk), BF16),
                        pltpu.VMEM((2, s // GLA_CHUNK, 8, hk), F32)],
        compiler_params=_cparams("parallel", "parallel"),
        name="gla_scan",
    )(proj, proj, proj, proj, small, w2f, b2f.reshape(1, -1), w2b, b2b.reshape(1, -1),
      norm_w.reshape(1, hv))


def _merge_kernel(a_ref, y_ref, wa_ref, wb_ref, gla_ref, glb_ref, ba_ref, bb_ref, o_ref):
    ya = jnp.dot(a_ref[0], wa_ref[...], preferred_element_type=F32)
    yb = jnp.dot(y_ref[0], wb_ref[...], preferred_element_type=F32)
    ga = jax.nn.sigmoid(gla_ref[0].astype(F32) + ba_ref[...])
    gb = jax.nn.sigmoid(glb_ref[0].astype(F32) + bb_ref[...])
    o_ref[0] = (ga * ya + gb * yb).astype(o_ref.dtype)


def _merge(a_n, y_n, w_a, w_b, proj, gl0, b_gate):
    b, l, ka = a_n.shape
    kb = y_n.shape[2]
    d = w_a.shape[1]
    bm = _pick(l, (1024, 512, 256))
    bn = 512
    nb = d // bn
    bg = b_gate.reshape(1, 2 * d)
    return pl.pallas_call(
        _merge_kernel,
        grid=(b, l // bm, nb),
        in_specs=[pl.BlockSpec((1, bm, ka), lambda i, t, j: (i, t, 0)),
                  pl.BlockSpec((1, bm, kb), lambda i, t, j: (i, t, 0)),
                  pl.BlockSpec((ka, bn), lambda i, t, j: (0, j)),
                  pl.BlockSpec((kb, bn), lambda i, t, j: (0, j)),
                  pl.BlockSpec((1, bm, bn), lambda i, t, j: (i, t, gl0 // bn + j)),
                  pl.BlockSpec((1, bm, bn), lambda i, t, j: (i, t, gl0 // bn + nb + j)),
                  pl.BlockSpec((1, bn), lambda i, t, j: (0, j)),
                  pl.BlockSpec((1, bn), lambda i, t, j: (0, nb + j))],
        out_specs=pl.BlockSpec((1, bm, bn), lambda i, t, j: (i, t, j)),
        out_shape=jax.ShapeDtypeStruct((b, l, d), BF16),
        compiler_params=_cparams("parallel", "parallel", "parallel"),
        name="branch_merge",
    )(a_n, y_n, w_a, w_b, proj, proj, bg, bg)


def _resid_kernel(m_ref, w_ref, x_ref, g_ref, o_ref):
    y = jnp.dot(m_ref[0], w_ref[...], preferred_element_type=F32)
    o_ref[0] = x_ref[0] + g_ref[0] * y


def _resid(m, w_o, x, gate):
    b, l, d = x.shape
    bm = _pick(l, (512, 256))
    bn = d
    return pl.pallas_call(
        _resid_kernel,
        grid=(b, l // bm, d // bn),
        in_specs=[pl.BlockSpec((1, bm, d), lambda i, t, j: (i, t, 0)),
                  pl.BlockSpec((d, bn), lambda i, t, j: (0, j)),
                  pl.BlockSpec((1, bm, bn), lambda i, t, j: (i, t, j)),
                  pl.BlockSpec((1, 1, bn), lambda i, t, j: (i, 0, j))],
        out_specs=pl.BlockSpec((1, bm, bn), lambda i, t, j: (i, t, j)),
        out_shape=jax.ShapeDtypeStruct((b, l, d), F32),
        compiler_params=_cparams("parallel", "parallel", "parallel"),
        name="attn_resid",
    )(m, w_o, x, gate)


def _peer_q_kernel(x_ref, nw_ref, sh_ref, sc_ref, wq_ref, h_ref, q_ref, lhs_ref):
    @pl.when(pl.program_id(2) == 0)
    def _():
        xf = x_ref[0]
        ms = jnp.mean(xf * xf, axis=-1, keepdims=True)
        y = xf * lax.rsqrt(ms + EPS) * nw_ref[...]
        hb = (y * (1.0 + sc_ref[0]) + sh_ref[0]).astype(BF16)
        lhs_ref[...] = hb
        h_ref[0] = hb

    q_ref[0] = jnp.dot(lhs_ref[...], wq_ref[...], preferred_element_type=F32)


def _peer_q(x1, norm_w, shift, scale, wq):
    b, l, d = x1.shape
    n = wq.shape[1]
    bm = _pick(l, (512, 256))
    bn = n
    return pl.pallas_call(
        _peer_q_kernel,
        grid=(b, l // bm, n // bn),
        in_specs=[pl.BlockSpec((1, bm, d), lambda i, t, j: (i, t, 0)),
                  pl.BlockSpec((1, d), lambda i, t, j: (0, 0)),
                  pl.BlockSpec((1, 1, d), lambda i, t, j: (i, 0, 0)),
                  pl.BlockSpec((1, 1, d), lambda i, t, j: (i, 0, 0)),
                  pl.BlockSpec((d, bn), lambda i, t, j: (0, j))],
        out_specs=[pl.BlockSpec((1, bm, d), lambda i, t, j: (i, t, 0)),
                   pl.BlockSpec((1, bm, bn), lambda i, t, j: (i, t, j))],
        out_shape=[jax.ShapeDtypeStruct((b, l, d), BF16),
                   jax.ShapeDtypeStruct((b, l, n), F32)],
        scratch_shapes=[pltpu.VMEM((bm, d), BF16)],
        compiler_params=_cparams("parallel", "parallel", "arbitrary"),
        name="peer_query",
    )(x1, norm_w.reshape(1, d), shift, scale, wq)


def _top_values(x, k, rows):
    sub = lax.broadcasted_iota(jnp.int32, (rows, x.shape[1]), 0)
    out = jnp.full((rows, x.shape[1]), -jnp.inf, F32)
    for i in range(k):
        m = jnp.max(x, axis=0, keepdims=True)
        out = jnp.where(sub == i, m, out)
        x = jnp.where(x == m, -jnp.inf, x)
    return out


def _peer_score_kernel(q_ref, keys_ref, c1_ref, e1_ref, s2_ref, e2_ref):
    k = PEER_TOPK
    dk = keys_ref.shape[3]
    q = q_ref[...]
    s1 = lax.dot_general(keys_ref[0, 0], q[:, 0:dk], NT_DIMS, precision=HIGHEST,
                         preferred_element_type=F32)
    s2 = lax.dot_general(keys_ref[0, 1], q[:, dk:2 * dk], NT_DIMS, precision=HIGHEST,
                         preferred_element_type=F32)
    s2_ref[0] = s2
    n = k + 1
    pad = -(-n // 8) * 8
    assert n // 2 <= 8 and n // 9 == 1
    for lc in range(q.shape[0] // LANES):
        ls = slice(lc * LANES, (lc + 1) * LANES)
        s1s, s2s = s1[:, ls], s2[:, ls]
        sv1 = _top_values(s1s, n, pad)
        sv2 = _top_values(s2s, n, pad)
        cand = jnp.concatenate(
            [sv1[0:1, :] + sv2]
            + [sv1[a:a + 1, :] + sv2[0:8, :] for a in range(1, 8)]
            + [sv1[8:pad, :] + sv2[0:1, :]], axis=0)
        cv = _top_values(cand, n, pad)
        z = jnp.sum(jnp.exp(cv[0:k, :] - cv[0:1, :]), axis=0, keepdims=True)
        thr = 0.5 * (cv[k - 1:k, :] + cv[k:k + 1, :])
        c1_ref[0, :, ls] = thr - s1s
        e1_ref[0, :, ls] = jnp.exp(s1s - sv1[0:1, :]) / z
        e2_ref[0, :, ls] = jnp.exp(s2s - sv2[0:1, :])


def _peer_scores(q, keys):
    t, n = q.shape
    h, _, nk, dk = keys.shape
    tt = _pick(t, (512, 256))
    big = jax.ShapeDtypeStruct((h, nk, t), F32)
    big_spec = pl.BlockSpec((1, nk, tt), lambda i, j: (j, 0, i))
    return pl.pallas_call(
        _peer_score_kernel,
        grid=(t // tt, h),
        in_specs=[pl.BlockSpec((tt, 2 * dk), lambda i, j: (i, j)),
                  pl.BlockSpec((1, 2, nk, dk), lambda i, j: (j, 0, 0, 0))],
        out_specs=[big_spec, big_spec, big_spec, big_spec],
        out_shape=[big, big, big, big],
        compiler_params=_cparams("parallel", "parallel"),
        name="peer_scores",
    )(q, keys)


PEER_EXPERT_BLOCK = 1024
GATE_ROWS = 8


def _peer_expert_kernel(h_ref, u_ref, vt_ref, c1_ref, e1_ref, s2_ref, e2_ref,
                        o_ref, ht_ref, w_ref, c8_ref, e8_ref, *, n_sub):
    j = pl.program_id(1)
    nh, nk, bm = s2_ref.shape
    sub = 8

    @pl.when(j == 0)
    def _():
        o_ref[...] = jnp.zeros_like(o_ref)
        ht_ref[...] = h_ref[...].astype(F32).T.astype(BF16)

    i1_0 = pl.multiple_of(j * n_sub, n_sub)
    for hd in range(nh):
        cgrp = c1_ref[hd, pl.ds(i1_0, n_sub), :]
        egrp = e1_ref[hd, pl.ds(i1_0, n_sub), :]
        for sb in range(n_sub):
            r0 = (hd * n_sub + sb) * sub
            c8_ref[r0:r0 + sub, :] = jnp.broadcast_to(cgrp[sb:sb + 1, :], (sub, bm))
            e8_ref[r0:r0 + sub, :] = jnp.broadcast_to(egrp[sb:sb + 1, :], (sub, bm))

    rows_per = u_ref.shape[0] // 2
    acts = [jnp.dot(u_ref[i * rows_per:(i + 1) * rows_per, :], ht_ref[...], preferred_element_type=F32)
            for i in range(2)]

    n_part = GATE_ROWS // sub

    for lc in range(bm // LANES):
        ls = slice(lc * LANES, (lc + 1) * LANES)
        for rc in range(nk // GATE_ROWS):
            base = rc * GATE_ROWS
            accs = [[jnp.zeros((sub, LANES), F32) for _ in range(n_part)] for _ in range(n_sub)]
            for hd in range(nh):
                s2v = [s2_ref[hd, base + k * sub:base + (k + 1) * sub, ls] for k in range(n_part)]
                e2v = [e2_ref[hd, base + k * sub:base + (k + 1) * sub, ls] for k in range(n_part)]
                for sb in range(n_sub):
                    r0 = (hd * n_sub + sb) * sub
                    cut = c8_ref[r0:r0 + sub, ls]
                    e1v = e8_ref[r0:r0 + sub, ls]
                    for k in range(n_part):
                        accs[sb][k] = accs[sb][k] + jnp.where(s2v[k] >= cut, e2v[k], 0.0) * e1v
            for sb in range(n_sub):
                for k in range(n_part):
                    r0 = sb * nk + base + k * sub
                    w_ref[r0:r0 + sub, ls] = accs[sb][k]

    parts = []
    for i, act in enumerate(acts):
        gelu = 0.5 * act * (1.0 + lax.erf(act * (2.0 ** -0.5)))
        parts.append((gelu * w_ref[i * rows_per:(i + 1) * rows_per, :]).astype(BF16))
    pmat = jnp.concatenate(parts, axis=0)
    o_ref[...] += jnp.dot(vt_ref[...], pmat, preferred_element_type=F32)


def _peer_experts(h2, u, vt, c1, e1, s2, e2):
    t, d = h2.shape
    e = u.shape[0]
    nh, nk, _ = s2.shape
    bm = _pick(t, (512, 256))
    eb = PEER_EXPERT_BLOCK
    n_sub = eb // nk
    kern = functools.partial(_peer_expert_kernel, n_sub=n_sub)
    big_spec = pl.BlockSpec((nh, nk, bm), lambda i, j: (0, 0, i))
    return pl.pallas_call(
        kern,
        grid=(t // bm, e // eb),
        in_specs=[pl.BlockSpec((bm, d), lambda i, j: (i, 0)),
                  pl.BlockSpec((eb, d), lambda i, j: (j, 0)),
                  pl.BlockSpec((d, eb), lambda i, j: (0, j)),
                  big_spec, big_spec, big_spec, big_spec],
        out_specs=pl.BlockSpec((d, bm), lambda i, j: (0, i)),
        out_shape=jax.ShapeDtypeStruct((d, t), F32),
        scratch_shapes=[pltpu.VMEM((d, bm), BF16),
                        pltpu.VMEM((eb, bm), F32),
                        pltpu.VMEM((nh * n_sub * 8, bm), F32),
                        pltpu.VMEM((nh * n_sub * 8, bm), F32)],
        compiler_params=_cparams("parallel", "arbitrary"),
        name="peer_experts",
    )(h2, u, vt, c1, e1, s2, e2)


def _final_kernel(x_ref, yt_ref, g_ref, w_ref, o_ref):
    xf = x_ref[0] + g_ref[0] * yt_ref[...].T
    ms = jnp.mean(xf * xf, axis=-1, keepdims=True)
    o_ref[0] = xf * lax.rsqrt(ms + EPS) * w_ref[...]


def _final(x1, yt, gate, w):
    b, l, d = x1.shape
    bm = _pick(l, (512, 256))
    nt = l // bm
    return pl.pallas_call(
        _final_kernel,
        grid=(b, l // bm),
        in_specs=[pl.BlockSpec((1, bm, d), lambda i, t: (i, t, 0)),
                  pl.BlockSpec((d, bm), lambda i, t: (0, i * nt + t)),
                  pl.BlockSpec((1, 1, d), lambda i, t: (i, 0, 0)),
                  pl.BlockSpec((1, d), lambda i, t: (0, 0))],
        out_specs=pl.BlockSpec((1, bm, d), lambda i, t: (i, t, 0)),
        out_shape=jax.ShapeDtypeStruct((b, l, d), F32),
        compiler_params=_cparams("parallel", "parallel"),
        name="final_norm",
    )(x1, yt, gate, w.reshape(1, d))


def _layer(x, ctx, mod_x, mod_c, p):
    b, l, d = x.shape
    lc = ctx.shape[1]
    s = l + lc
    dk = p['w_lr2_f'].shape[1]
    dv = p['w_gla_out'].shape[0]
    di = p['w_ssd_out'].shape[0]
    bc = SSD_GROUPS * SSD_STATE
    n_ssd_heads = p['a_log_f'].shape[0]

    sizes = (dk, dk, dv, dv, GLA_RANK, GLA_RANK, di, di + 2 * bc, n_ssd_heads, n_ssd_heads, 2 * d)
    offs = [0]
    for sz in sizes:
        offs.append(offs[-1] + sz)
    w_in = p['w_in']
    seg = lambda i: w_in[:, offs[i]:offs[i + 1]]
    w_main = jnp.concatenate([seg(0), seg(1), seg(2), seg(3), seg(6), seg(7), seg(10)], axis=1).astype(BF16)
    n_small = 2 * LANES
    w_small = jnp.concatenate(
        [seg(8), seg(9), seg(4), seg(5),
         jnp.zeros((d, n_small - 2 * n_ssd_heads - 2 * GLA_RANK), w_in.dtype)], axis=1).astype(BF16)
    q0, k0, v0, r0 = 0, dk, 2 * dk, 2 * dk + dv
    z0 = r0 + dv
    xbc0 = z0 + di
    gl0 = xbc0 + di + 2 * bc

    shift1 = jnp.stack([mod_x[0], jnp.broadcast_to(mod_c[0], (b, d))], axis=1).reshape(b, 2, 1, d)
    scale1 = jnp.stack([mod_x[1], jnp.broadcast_to(mod_c[1], (b, d))], axis=1).reshape(b, 2, 1, d)
    h = _norm_mod(x, ctx, p['norm1_w'], shift1, scale1)

    h2d = h.reshape(b * s, d)
    proj = _matmul(h2d, w_main, BF16, "in_proj").reshape(b, s, -1)
    small = _matmul(h2d, w_small, F32, "in_proj_gates").reshape(b, s, n_small)

    xbc = _conv(proj, xbc0 // 512, di + 2 * bc, p['conv_w'], p['conv_b'], l, lc)
    bias = jnp.concatenate([p['dt_bias_f'], p['dt_bias_b']]).reshape(1, LANES).astype(F32)
    a_neg = -jnp.exp(jnp.concatenate([p['a_log_f'], p['a_log_b']]).astype(F32)).reshape(1, LANES)
    dtr, cumr, dt3, cum3 = _ssd_prep(small, bias, a_neg)
    g, hpg, c = SSD_GROUPS, SSD_HPG, SSD_CHUNK
    per_dir = lambda t: t.reshape(b, 2, g, s // c, hpg, c)
    dskip = jnp.repeat(p['d_skip'].astype(F32), SSD_HEADDIM).reshape(g, 1, hpg * SSD_HEADDIM)
    ssd_nw = p['ssd_norm_w'].astype(F32).reshape(g, 1, hpg * SSD_HEADDIM)
    y_n = _ssd(xbc, proj, z0 // 512, _split3_cols(cum3, b, s), _split3_cols(dt3, b, s),
               per_dir(cumr), per_dir(dtr), dskip, ssd_nw, l, lc)

    a_n = _gla(proj, small, p['w_lr2_f'], p['b_lr_f'], p['w_lr2_b'], p['b_lr_b'], p['gla_norm_w'],
               l, lc, (q0, k0, v0, r0))

    m = _merge(a_n, y_n, p['w_gla_out'].astype(BF16), p['w_ssd_out'].astype(BF16), proj, gl0,
               p['b_gate'])
    return _resid(m, p['w_o'].astype(BF16), x, mod_x[2].reshape(b, 1, d))


def _peer(x1, mod_x, norm_w, wq, keys, u, v):
    b, l, d = x1.shape
    h2, q = _peer_q(x1, norm_w, mod_x[3].reshape(b, 1, d), mod_x[4].reshape(b, 1, d), wq.astype(BF16))
    c1, e1, s2, e2 = _peer_scores(q.reshape(b * l, -1), keys)
    return _peer_experts(h2.reshape(b * l, d), u.astype(BF16), v.astype(BF16).T, c1, e1, s2, e2)


def kernel(x, c, ctx, c_ctx, w_ada, b_ada, norm1_w, w_in, b_gate, w_lr2_f, b_lr_f, w_lr2_b, b_lr_b,
           gla_norm_w, w_gla_out, conv_w, conv_b, a_log_f, a_log_b, dt_bias_f, dt_bias_b, d_skip,
           ssd_norm_w, w_ssd_out, w_o, norm2_w, peer_wq, peer_keys, peer_u, peer_v, final_norm_w):
    b, l, d = x.shape
    depth = w_in.shape[0]
    assert depth == 1, "context-stream update for deeper stacks is not implemented"
    layer = 0
    rows = -(-(b + 1) // 8) * 8
    c_all = jnp.concatenate([c, c_ctx[None, :], jnp.zeros((rows - b - 1, d), c.dtype)], axis=0)
    mod = _ada(c_all, w_ada[layer], b_ada[layer])
    mod_x = [mod[:b, i * d:(i + 1) * d] for i in range(N_MOD)]
    mod_c = [mod[b, i * d:(i + 1) * d] for i in range(N_MOD)]
    p = {
        'norm1_w': norm1_w[layer], 'w_in': w_in[layer], 'b_gate': b_gate[layer],
        'w_lr2_f': w_lr2_f[layer], 'b_lr_f': b_lr_f[layer],
        'w_lr2_b': w_lr2_b[layer], 'b_lr_b': b_lr_b[layer],
        'gla_norm_w': gla_norm_w[layer], 'w_gla_out': w_gla_out[layer],
        'conv_w': conv_w[layer], 'conv_b': conv_b[layer],
        'a_log_f': a_log_f[layer], 'a_log_b': a_log_b[layer],
        'dt_bias_f': dt_bias_f[layer], 'dt_bias_b': dt_bias_b[layer],
        'd_skip': d_skip[layer], 'ssd_norm_w': ssd_norm_w[layer],
        'w_ssd_out': w_ssd_out[layer], 'w_o': w_o[layer],
    }
    x1 = _layer(x, ctx, mod_x, mod_c, p)
    y = _peer(x1, mod_x, norm2_w[layer], peer_wq[layer], peer_keys[layer], peer_u[layer], peer_v[layer])
    return _final(x1, y, mod_x[5].reshape(b, 1, d), final_norm_w)
```
